```python
import math
import jax, jax.numpy as jnp
from jax import lax
import numpy as np

D_MODEL = 1024
BATCH = 8
SEQ = 2048
DEPTH = 1
DEC_BATCH = 128
DEC_SEQ = 8
PAST_LEN = 16384
PAGE_SIZE = 128

MIX_WIDTH = D_MODEL
A_GROUPS = 4
A_WIDTH = MIX_WIDTH // 2
A_CH = A_WIDTH // A_GROUPS
CHUNK = 128
R_HEADS = 4
R_WIDTH = MIX_WIDTH - A_WIDTH
R_DK = R_WIDTH // R_HEADS
R_DV = R_WIDTH // R_HEADS
RET_CHUNK = 128
ROPE_BASE = 10000.0
IN_WIDTH = 2 * A_WIDTH + 4 * R_WIDTH
N_MEM = 256
CA_HEADS = 4
CA_DH = D_MODEL // CA_HEADS
PEER_HEADS = 8
PEER_KEYS = 128
PEER_EXPERTS = PEER_KEYS * PEER_KEYS
PEER_TOPK = 16
PEER_DKEY = 256
PEER_DHALF = PEER_DKEY // 2
PEER_BLOCK = 128
ALPHA = (2 * DEPTH) ** 0.25
BETA = (8 * DEPTH) ** -0.25
LN_EPS = 1e-5

kernel_name = 'hybrid_gmlp_retention_peer_step'


def layer_norm(x, g, b):
    xf = x.astype(jnp.float32)
    mu = xf.mean(-1, keepdims=True)
    var = jnp.square(xf - mu).mean(-1, keepdims=True)
    return ((xf - mu) * lax.rsqrt(var + LN_EPS)).astype(x.dtype) * g + b


def rotary(x, pos):
    half = x.shape[-1] // 2
    inv = 1.0 / (ROPE_BASE ** (jnp.arange(half, dtype=jnp.float32) / half))
    ang = pos.astype(jnp.float32)[:, None] * inv[None, :]
    cos = jnp.cos(ang)[None, :, None, :]
    sin = jnp.sin(ang)[None, :, None, :]
    x1, x2 = x[..., :half], x[..., half:]
    return jnp.concatenate([x1 * cos - x2 * sin, x1 * sin + x2 * cos], axis=-1)


def spatial_gate(u, v, w_s, b_s, g, b):
    B, L = v.shape[0], v.shape[1]
    v = layer_norm(v, g, b)
    Lp = -(-L // CHUNK) * CHUNK
    vp = jnp.pad(v, ((0, 0), (0, Lp - L), (0, 0), (0, 0))).reshape(B, Lp // CHUNK, CHUNK, A_GROUPS, A_CH)
    ws = w_s * jnp.tril(jnp.ones((CHUNK, CHUNK), w_s.dtype))
    mixed = jnp.einsum('gts,bnsgc->bntgc', ws, vp) + b_s.T[None, None, :, :, None]
    mixed = mixed.reshape(B, Lp, A_GROUPS, A_CH)[:, :L]
    return u * mixed, v


def retention(q, k, v, s0):
    B, L, H = q.shape[0], q.shape[1], q.shape[2]
    C = RET_CHUNK if L % RET_CHUNK == 0 else L
    n = L // C
    lg = jnp.log(1.0 - 2.0 ** (-5.0 - jnp.arange(H, dtype=jnp.float32)))
    i = jnp.arange(C, dtype=jnp.float32)
    diff = i[:, None] - i[None, :]
    dmask = jnp.where(diff[None] >= 0, jnp.exp(jnp.maximum(diff, 0.0)[None] * lg[:, None, None]), 0.0)
    q_dec = jnp.exp((i + 1.0)[:, None] * lg[None, :])
    k_dec = jnp.exp((C - 1.0 - i)[:, None] * lg[None, :])
    c_dec = jnp.exp(C * lg)

    def step(S, blk):
        qc, kc, vc = blk
        inner = jnp.einsum('bihd,bjhd->bhij', qc, kc) * dmask
        o = (jnp.einsum('bhij,bjhe->bihe', inner, vc)
             + jnp.einsum('bihd,bhde->bihe', qc, S) * q_dec[None, :, :, None])
        S = S * c_dec[None, :, None, None] + jnp.einsum('bjhd,bjhe->bhde', kc * k_dec[None, :, :, None], vc)
        return S, o

    def split(t):
        return t.reshape(B, n, C, H, t.shape[-1]).transpose(1, 0, 2, 3, 4)

    S, o = lax.scan(step, s0, (split(q), split(k), split(v)))
    o = o.transpose(1, 0, 2, 3, 4).reshape(B, L, H, v.shape[-1])
    return o, S


def peer(x, wq, subkeys, u_tab, v_tab):
    T, D = x.shape
    pad = (-T) % PEER_BLOCK
    xb = jnp.pad(x, ((0, pad), (0, 0))).reshape(-1, PEER_BLOCK, D)
    ncand = PEER_TOPK * PEER_TOPK

    def block(xt):
        q = (xt @ wq).reshape(PEER_BLOCK, PEER_HEADS, 2, PEER_DHALF)
        s = jnp.einsum('thcd,hckd->thck', q, subkeys).astype(jnp.float32)
        sv, si = lax.top_k(s, PEER_TOPK)
        cand = (sv[:, :, 0, :, None] + sv[:, :, 1, None, :]).reshape(PEER_BLOCK, PEER_HEADS, ncand)
        cidx = (si[:, :, 0, :, None] * PEER_KEYS + si[:, :, 1, None, :]).reshape(PEER_BLOCK, PEER_HEADS, ncand)
        fv, fi = lax.top_k(cand, PEER_TOPK)
        eidx = jnp.take_along_axis(cidx, fi, axis=-1)
        g = jax.nn.softmax(fv, axis=-1).astype(xt.dtype)
        act = jax.nn.gelu(jnp.einsum('thed,td->the', u_tab[eidx], xt))
        return jnp.einsum('the,thed->td', g * act, v_tab[eidx])

    return lax.map(block, xb).reshape(-1, D)[:T]


def layer(x, pos0, mem_k, mem_v, ret_s0, w_in, w_s, b_s, gate_ln_g, gate_ln_b, ret_gn_g, ret_gn_b,
          w_o, ln1_g, ln1_b, ca_wq, ca_wo, ln2_g, ln2_b, peer_wq, peer_subkeys, peer_u, peer_v,
          ln3_g, ln3_b):
    B, L, D = x.shape
    f32 = jnp.float32
    h = x @ w_in
    cuts = [A_WIDTH, 2 * A_WIDTH, 2 * A_WIDTH + R_WIDTH, 2 * A_WIDTH + 2 * R_WIDTH, 2 * A_WIDTH + 3 * R_WIDTH]
    ua, va, qr, kr, vr, gr = jnp.split(h, cuts, axis=-1)
    ua = jax.nn.gelu(ua).reshape(B, L, A_GROUPS, A_CH)
    va = jax.nn.gelu(va).reshape(B, L, A_GROUPS, A_CH)
    a_out, va_n = spatial_gate(ua, va, w_s, b_s, gate_ln_g, gate_ln_b)
    pos = pos0 + jnp.arange(L, dtype=jnp.int32)
    q = rotary(qr.reshape(B, L, R_HEADS, R_DK).astype(f32), pos)
    k = rotary(kr.reshape(B, L, R_HEADS, R_DK).astype(f32), pos) * (R_DK ** -0.5)
    v = vr.reshape(B, L, R_HEADS, R_DV).astype(f32)
    o, s_new = retention(q, k, v, ret_s0.astype(f32))
    o = layer_norm(o, ret_gn_g.astype(f32), ret_gn_b.astype(f32)).astype(x.dtype)
    r_out = jax.nn.silu(gr) * o.reshape(B, L, R_WIDTH)
    mix = jnp.concatenate([a_out.reshape(B, L, A_WIDTH), r_out], axis=-1) @ w_o
    x = layer_norm(ALPHA * x + mix, ln1_g, ln1_b)
    qc = (x @ ca_wq).reshape(B, L, CA_HEADS, CA_DH)
    sc = jnp.einsum('blhd,bmhd->bhlm', qc, mem_k).astype(f32) * (CA_DH ** -0.5)
    p = jax.nn.softmax(sc, axis=-1).astype(x.dtype)
    ca = jnp.einsum('bhlm,bmhd->blhd', p, mem_v).reshape(B, L, D) @ ca_wo
    x = layer_norm(ALPHA * x + ca, ln2_g, ln2_b)
    f = peer(x.reshape(B * L, D), peer_wq, peer_subkeys, peer_u, peer_v).reshape(B, L, D)
    x = layer_norm(ALPHA * x + f, ln3_g, ln3_b)
    return x, s_new.astype(ret_s0.dtype), va_n


def setup_inputs(seed: int = 0) -> dict:
    key = jax.random.key(seed)
    ks = jax.random.split(key, 32)
    f32 = jnp.float32

    def nrm(i, shape, scale):
        return jax.random.normal(ks[i], shape, f32) * scale

    def gain(i, shape):
        return 1.0 + nrm(i, shape, 0.02)

    D = D_MODEL
    return {
        'x_prompt': nrm(0, (BATCH, SEQ, D), 1.0),
        'x_sample': nrm(1, (DEC_BATCH, DEC_SEQ, D), 1.0),
        'mem_prompt': nrm(2, (BATCH, N_MEM, D), 1.0),
        'cache_mem_k': nrm(3, (DEPTH, DEC_BATCH, N_MEM, CA_HEADS, CA_DH), 1.0),
        'cache_mem_v': nrm(4, (DEPTH, DEC_BATCH, N_MEM, CA_HEADS, CA_DH), 1.0),
        'state_ret': nrm(5, (DEPTH, DEC_BATCH, R_HEADS, R_DK, R_DV), 0.1),
        'w_in': nrm(6, (DEPTH, D, IN_WIDTH), D ** -0.5),
        'w_s': nrm(7, (DEPTH, A_GROUPS, CHUNK, CHUNK), CHUNK ** -0.5),
        'b_s': gain(8, (DEPTH, A_GROUPS, CHUNK)),
        'gate_ln_g': gain(9, (DEPTH, A_GROUPS, A_CH)),
        'gate_ln_b': nrm(10, (DEPTH, A_GROUPS, A_CH), 0.02),
        'ret_gn_g': gain(11, (DEPTH, R_HEADS, R_DV)),
        'ret_gn_b': nrm(12, (DEPTH, R_HEADS, R_DV), 0.02),
        'w_o': nrm(13, (DEPTH, MIX_WIDTH, D), BETA * MIX_WIDTH ** -0.5),
        'ln1_g': gain(14, (DEPTH, D)),
        'ln1_b': nrm(15, (DEPTH, D), 0.02),
        'ca_wq': nrm(16, (DEPTH, D, D), D ** -0.5),
        'ca_wk': nrm(17, (DEPTH, D, D), D ** -0.5),
        'ca_wv': nrm(18, (DEPTH, D, D), D ** -0.5),
        'ca_wo': nrm(19, (DEPTH, D, D), BETA * D ** -0.5),
        'ln2_g': gain(20, (DEPTH, D)),
        'ln2_b': nrm(21, (DEPTH, D), 0.02),
        'peer_wq': nrm(22, (DEPTH, D, PEER_HEADS * PEER_DKEY), D ** -0.5),
        'peer_subkeys': nrm(23, (DEPTH, PEER_HEADS, 2, PEER_KEYS, PEER_DHALF), PEER_DHALF ** -0.5),
        'peer_u': nrm(24, (DEPTH, PEER_EXPERTS, D), D ** -0.5),
        'peer_v': nrm(25, (DEPTH, PEER_EXPERTS, D), BETA * PEER_HEADS ** -0.5),
        'ln3_g': gain(26, (DEPTH, D)),
        'ln3_b': nrm(27, (DEPTH, D), 0.02),
    }


def reference(x_prompt, x_sample, mem_prompt, cache_mem_k, cache_mem_v, state_ret, w_in, w_s, b_s,
              gate_ln_g, gate_ln_b, ret_gn_g, ret_gn_b, w_o, ln1_g, ln1_b, ca_wq, ca_wk, ca_wv, ca_wo,
              ln2_g, ln2_b, peer_wq, peer_subkeys, peer_u, peer_v, ln3_g, ln3_b):
    yp, ys = x_prompt, x_sample
    bp = x_prompt.shape[0]
    mk_l, mv_l, sp_l, ss_l, gv_l = [], [], [], [], []
    for l in range(DEPTH):
        lw = (w_in[l], w_s[l], b_s[l], gate_ln_g[l], gate_ln_b[l], ret_gn_g[l], ret_gn_b[l], w_o[l],
              ln1_g[l], ln1_b[l], ca_wq[l], ca_wo[l], ln2_g[l], ln2_b[l], peer_wq[l], peer_subkeys[l],
              peer_u[l], peer_v[l], ln3_g[l], ln3_b[l])
        mem_k = (mem_prompt @ ca_wk[l]).reshape(bp, N_MEM, CA_HEADS, CA_DH)
        mem_v = (mem_prompt @ ca_wv[l]).reshape(bp, N_MEM, CA_HEADS, CA_DH)
        s0 = jnp.zeros((bp, R_HEADS, R_DK, R_DV), x_prompt.dtype)
        yp, sp, _ = layer(yp, 0, mem_k, mem_v, s0, *lw)
        ys, ss, gv = layer(ys, PAST_LEN, cache_mem_k[l], cache_mem_v[l], state_ret[l], *lw)
        mk_l.append(mem_k)
        mv_l.append(mem_v)
        sp_l.append(sp)
        ss_l.append(ss)
        gv_l.append(gv)
    new_mem_k = jnp.stack(mk_l)
    new_mem_v = jnp.stack(mv_l)
    new_ret_prompt = jnp.stack(sp_l)
    new_ret_sample = jnp.stack(ss_l)
    new_gate_v_sample = jnp.stack(gv_l)
    return (yp, ys, new_mem_k, new_mem_v, new_ret_prompt, new_ret_sample, new_gate_v_sample)
```

```python
import functools

import jax
import jax.numpy as jnp
from jax import lax
from jax.experimental import pallas as pl
from jax.experimental.pallas import tpu as pltpu

F32 = jnp.float32
BF16 = jnp.bfloat16

A_GROUPS = 4
A_CH = 128
CHUNK = 128
R_HEADS = 4
R_DK = 128
ROPE_BASE = 10000.0
N_MEM = 256
CA_HEADS = 4
CA_DH = 256
PEER_HEADS = 8
PEER_KEYS = 128
PEER_TOPK = 16
LN_EPS = 1e-5
TILE = 128
VMEM_LIMIT = 48 * 1024 * 1024


def _params(*sem):
    return pltpu.CompilerParams(dimension_semantics=sem, vmem_limit_bytes=VMEM_LIMIT)


def _ln(x, g, b):
    mu = jnp.mean(x, axis=-1, keepdims=True)
    xc = x - mu
    var = jnp.mean(xc * xc, axis=-1, keepdims=True)
    return xc * lax.rsqrt(var + LN_EPS) * g + b


def _dot(a, b):
    return jnp.dot(a, b, preferred_element_type=F32)


def _dot_nt(a, b):
    return lax.dot_general(a, b, (((1,), (1,)), ((), ())), preferred_element_type=F32)


def _mm_kernel(x_ref, w_ref, o_ref):
    o_ref[...] = _dot(x_ref[...].astype(BF16), w_ref[...])


def _mm(x, w, tm, tn):
    t, k = x.shape
    n = w.shape[1]
    return pl.pallas_call(
        _mm_kernel,
        grid=(t // tm, n // tn),
        in_specs=[pl.BlockSpec((tm, k), lambda i, j: (i, 0)),
                  pl.BlockSpec((k, tn), lambda i, j: (0, j))],
        out_specs=pl.BlockSpec((tm, tn), lambda i, j: (i, j)),
        out_shape=jax.ShapeDtypeStruct((t, n), F32),
        compiler_params=_params("parallel", "arbitrary"),
        name="mm",
    )(x, w)


def _mm_res_ln_kernel(a_ref, w_ref, x_ref, g_ref, b_ref, o_ref, *, alpha):
    y = _dot(a_ref[...], w_ref[...])
    o_ref[...] = _ln(alpha * x_ref[...] + y, g_ref[...], b_ref[...])


def _mm_res_ln(a, w, x, g, b, alpha, tm):
    t, k = a.shape
    d = w.shape[1]
    return pl.pallas_call(
        functools.partial(_mm_res_ln_kernel, alpha=alpha),
        grid=(t // tm,),
        in_specs=[pl.BlockSpec((tm, k), lambda i: (i, 0)),
                  pl.BlockSpec((k, d), lambda i: (0, 0)),
                  pl.BlockSpec((tm, d), lambda i: (i, 0)),
                  pl.BlockSpec((1, d), lambda i: (0, 0)),
                  pl.BlockSpec((1, d), lambda i: (0, 0))],
        out_specs=pl.BlockSpec((tm, d), lambda i: (i, 0)),
        out_shape=jax.ShapeDtypeStruct((t, d), F32),
        compiler_params=_params("parallel"),
        name="mm_res_ln",
    )(a, w, x, g.reshape(1, d), b.reshape(1, d))


def _mixer_consts(lc, pos):
    half = R_DK // 2
    inv = 1.0 / (ROPE_BASE ** (jnp.arange(half, dtype=F32) / half))
    ang = pos.astype(F32)[:, None] * inv[None, :]
    cos, sin = jnp.cos(ang), jnp.sin(ang)
    rot_c = jnp.concatenate([cos, cos], axis=-1)
    rot_s = jnp.concatenate([-sin, sin], axis=-1)
    lg = jnp.log(1.0 - 2.0 ** (-5.0 - jnp.arange(R_HEADS, dtype=F32)))
    r = jnp.arange(TILE)
    i = (r % lc).astype(F32)
    same = (r[:, None] // lc) == (r[None, :] // lc)
    diff = i[:, None] - i[None, :]
    dm = jnp.where(same[None] & (diff[None] >= 0),
                   jnp.exp(jnp.maximum(diff, 0.0)[None] * lg[:, None, None]), 0.0)
    ones = jnp.ones((1, 1, TILE), F32)
    qd = jnp.exp((i + 1.0)[None, :] * lg[:, None])[:, :, None] * ones
    kd = jnp.exp((lc - 1.0 - i)[None, :] * lg[:, None])[:, :, None] * ones
    cd = jnp.exp(lc * lg)[:, None, None] * jnp.ones((1, 8, TILE), F32)
    return rot_c, rot_s, dm, qd, kd, cd


def _rot(x, c, s):
    return x * c + pltpu.roll(x, R_DK // 2, 1) * s


def _gate_part(h_ref, ws_ref, bsb_ref, glg_ref, glb_ref, mix_ref, van_ref, lc):
    r = lax.broadcasted_iota(jnp.int32, (TILE, TILE), 0)
    c = lax.broadcasted_iota(jnp.int32, (TILE, TILE), 1)
    mask = (c <= r) & ((r // lc) == (c // lc))
    half = A_GROUPS * A_CH
    for g in range(A_GROUPS):
        sl = slice(g * A_CH, (g + 1) * A_CH)
        u = jax.nn.gelu(h_ref[:, sl])
        v = jax.nn.gelu(h_ref[:, half + g * A_CH: half + (g + 1) * A_CH])
        vn = _ln(v, glg_ref[g:g + 1, :], glb_ref[g:g + 1, :])
        w = jnp.where(mask, ws_ref[g], 0.0).astype(BF16)
        mixed = _dot(w, vn.astype(BF16)) + bsb_ref[g]
        mix_ref[:, sl] = (u * mixed).astype(BF16)
        if van_ref is not None:
            van_ref[:, sl] = vn


def _ret_head_inputs(h_ref, hd, rc, rs):
    base = 2 * A_GROUPS * A_CH
    w = R_HEADS * R_DK
    sl = lambda j: slice(base + j * w + hd * R_DK, base + j * w + (hd + 1) * R_DK)
    q = _rot(h_ref[:, sl(0)], rc, rs)
    k = _rot(h_ref[:, sl(1)], rc, rs) * (R_DK ** -0.5)
    return q, k, h_ref[:, sl(2)], h_ref[:, sl(3)]


def _ret_finish(o, gr, gng_ref, gnb_ref, mix_ref, hd):
    on = _ln(o, gng_ref[hd:hd + 1, :], gnb_ref[hd:hd + 1, :])
    off = A_GROUPS * A_CH + hd * R_DK
    mix_ref[:, off:off + R_DK] = (jax.nn.silu(gr) * on).astype(BF16)


def _mixer_prompt_kernel(h_ref, rc_ref, rs_ref, ws_ref, bsb_ref, glg_ref, glb_ref, dm_ref, qd_ref,
                         kd_ref, cd_ref, gng_ref, gnb_ref, mix_ref, s_ref):
    @pl.when(pl.program_id(1) == 0)
    def _():
        s_ref[...] = jnp.zeros_like(s_ref)

    _gate_part(h_ref, ws_ref, bsb_ref, glg_ref, glb_ref, mix_ref, None, CHUNK)
    rc, rs = rc_ref[...], rs_ref[...]
    for hd in range(R_HEADS):
        q, k, v, gr = _ret_head_inputs(h_ref, hd, rc, rs)
        qb, kb, vb = q.astype(BF16), k.astype(BF16), v.astype(BF16)
        inner = _dot_nt(qb, kb) * dm_ref[hd]
        s = s_ref[0, hd]
        o = _dot(inner.astype(BF16), vb) + _dot(qb, s.astype(BF16)) * qd_ref[hd]
        kt = (k * kd_ref[hd]).T.astype(BF16)
        s_ref[0, hd] = s * cd_ref[hd, 0:1, :] + _dot(kt, vb)
        _ret_finish(o, gr, gng_ref, gnb_ref, mix_ref, hd)


def _mixer_sample_kernel(h_ref, rc_ref, rs_ref, ws_ref, bsb_ref, glg_ref, glb_ref, dm_ref, qd_ref,
                         kd_ref, cd_ref, gng_ref, gnb_ref, s0_ref, mix_ref, s_ref, van_ref, *, lc):
    nb = TILE // lc
    _gate_part(h_ref, ws_ref, bsb_ref, glg_ref, glb_ref, mix_ref, van_ref, lc)
    rc, rs = rc_ref[...], rs_ref[...]
    rb = lax.broadcasted_iota(jnp.int32, (TILE, R_DK), 0) // lc
    for hd in range(R_HEADS):
        q, k, v, gr = _ret_head_inputs(h_ref, hd, rc, rs)
        qb, kb, vb = q.astype(BF16), k.astype(BF16), v.astype(BF16)
        inner = _dot_nt(qb, kb) * dm_ref[hd]
        cross = jnp.concatenate(
            [_dot(qb[bi * lc:(bi + 1) * lc], s0_ref[bi, hd].astype(BF16)) for bi in range(nb)], axis=0)
        o = _dot(inner.astype(BF16), vb) + cross * qd_ref[hd]
        kt = (k * kd_ref[hd]).T.astype(BF16)
        cd = cd_ref[hd, 0:1, :]
        for bi in range(nb):
            vm = jnp.where(rb == bi, v, 0.0).astype(BF16)
            s_ref[bi, hd] = s0_ref[bi, hd] * cd + _dot(kt, vm)
        _ret_finish(o, gr, gng_ref, gnb_ref, mix_ref, hd)


def _mixer(h, nseq, seqlen, pos0, s0, w_s, b_s, glg, glb, gng, gnb):
    t, win = h.shape
    dmix = A_GROUPS * A_CH + R_HEADS * R_DK
    if seqlen % CHUNK == 0:
        lc, nchunk = CHUNK, seqlen // CHUNK
        pos = pos0 + jnp.arange(seqlen, dtype=jnp.int32)
    else:
        assert TILE % seqlen == 0 and nseq % (TILE // seqlen) == 0 and s0 is not None
        lc = seqlen
        pos = pos0 + (jnp.arange(TILE, dtype=jnp.int32) % lc)
    rot_c, rot_s, dm, qd, kd, cd = _mixer_consts(lc, pos)
    reps = TILE // lc
    ws_t = jnp.tile(w_s[:, :lc, :lc], (1, reps, reps))
    bsb = jnp.tile(b_s[:, :lc], (1, reps))[:, :, None] * jnp.ones((1, 1, A_CH), F32)
    full = lambda shape: pl.BlockSpec(shape, lambda *_: (0,) * len(shape))
    const_specs = [full((A_GROUPS, TILE, TILE)), full((A_GROUPS, TILE, A_CH)),
                   full((A_GROUPS, A_CH)), full((A_GROUPS, A_CH)),
                   full((R_HEADS, TILE, TILE)), full((R_HEADS, TILE, R_DK)),
                   full((R_HEADS, TILE, R_DK)), full((R_HEADS, 8, R_DK)),
                   full((R_HEADS, R_DK)), full((R_HEADS, R_DK))]
    consts = (ws_t, bsb, glg, glb, dm, qd, kd, cd, gng, gnb)
    if lc == CHUNK:
        mix, s_new = pl.pallas_call(
            _mixer_prompt_kernel,
            grid=(nseq, nchunk),
            in_specs=[pl.BlockSpec((TILE, win), lambda b, n: (b * nchunk + n, 0)),
                      pl.BlockSpec((TILE, R_DK), lambda b, n: (n, 0)),
                      pl.BlockSpec((TILE, R_DK), lambda b, n: (n, 0))] + const_specs,
            out_specs=[pl.BlockSpec((TILE, dmix), lambda b, n: (b * nchunk + n, 0)),
                       pl.BlockSpec((1, R_HEADS, R_DK, R_DK), lambda b, n: (b, 0, 0, 0))],
            out_shape=[jax.ShapeDtypeStruct((t, dmix), BF16),
                       jax.ShapeDtypeStruct((nseq, R_HEADS, R_DK, R_DK), F32)],
            compiler_params=_params("parallel", "arbitrary"),
            name="mixer_prompt",
        )(h, rot_c, rot_s, *consts)
        return mix, s_new, None
    mix, s_new, van = pl.pallas_call(
        functools.partial(_mixer_sample_kernel, lc=lc),
        grid=(t // TILE,),
        in_specs=[pl.BlockSpec((TILE, win), lambda i: (i, 0)),
                  full((TILE, R_DK)), full((TILE, R_DK))] + const_specs
                 + [pl.BlockSpec((reps, R_HEADS, R_DK, R_DK), lambda i: (i, 0, 0, 0))],
        out_specs=[pl.BlockSpec((TILE, dmix), lambda i: (i, 0)),
                   pl.BlockSpec((reps, R_HEADS, R_DK, R_DK), lambda i: (i, 0, 0, 0)),
                   pl.BlockSpec((TILE, A_GROUPS * A_CH), lambda i: (i, 0))],
        out_shape=[jax.ShapeDtypeStruct((t, dmix), BF16),
                   jax.ShapeDtypeStruct((nseq, R_HEADS, R_DK, R_DK), F32),
                   jax.ShapeDtypeStruct((t, A_GROUPS * A_CH), F32)],
        compiler_params=_params("parallel"),
        name="mixer_sample",
    )(h, rot_c, rot_s, *consts, s0)
    return mix, s_new, van


def _attn_rows(q, k, v):
    outs = []
    for hh in range(CA_HEADS):
        sl = slice(hh * CA_DH, (hh + 1) * CA_DH)
        sc = _dot_nt(q[:, sl].astype(BF16), k[:, sl].astype(BF16)) * (CA_DH ** -0.5)
        e = jnp.exp(sc - jnp.max(sc, axis=-1, keepdims=True))
        p = e / jnp.sum(e, axis=-1, keepdims=True)
        outs.append(_dot(p.astype(BF16), v[:, sl].astype(BF16)))
    return outs


def _attn_kernel(q_ref, k_ref, v_ref, o_ref, *, nb, rows):
    for bi in range(nb):
        outs = _attn_rows(q_ref[bi * rows:(bi + 1) * rows, :], k_ref[bi], v_ref[bi])
        for hh, o in enumerate(outs):
            o_ref[bi * rows:(bi + 1) * rows, hh * CA_DH:(hh + 1) * CA_DH] = o.astype(BF16)


def _attn(qc, mem_k, mem_v, nseq, seqlen):
    t, d = qc.shape
    if seqlen >= 512:
        nb, rows = 1, 512
        per_seq = seqlen // rows
        grid = (nseq, per_seq)
        q_map = lambda b, i: (b * per_seq + i, 0)
        kv_map = lambda b, i: (b, 0, 0)
        sem = ("parallel", "arbitrary")
    else:
        nb, rows = 4, seqlen
        grid = (nseq // nb,)
        q_map = lambda i: (i, 0)
        kv_map = lambda i: (i, 0, 0)
        sem = ("parallel",)
    return pl.pallas_call(
        functools.partial(_attn_kernel, nb=nb, rows=rows),
        grid=grid,
        in_specs=[pl.BlockSpec((nb * rows, d), q_map),
                  pl.BlockSpec((nb, N_MEM, d), kv_map),
                  pl.BlockSpec((nb, N_MEM, d), kv_map)],
        out_specs=pl.BlockSpec((nb * rows, d), q_map),
        out_shape=jax.ShapeDtypeStruct((t, d), BF16),
        compiler_params=_params(*sem),
        name="attn",
    )(qc, mem_k, mem_v)


_ROW_LEN = [PEER_TOPK // (k1 + 1) for k1 in range(PEER_TOPK)]
_NCAND = sum(_ROW_LEN)
_NCAND_PAD = -(-_NCAND // 8) * 8


def _topk_rows(s, k):
    n = s.shape[0]
    iota = lax.broadcasted_iota(jnp.int32, s.shape, 0)
    vals, idxs = [], []
    for _ in range(k):
        m = jnp.max(s, axis=0, keepdims=True)
        idx = jnp.min(jnp.where(s == m, iota, n), axis=0, keepdims=True)
        vals.append(m)
        idxs.append(idx)
        s = jnp.where(iota == idx, -jnp.inf, s)
    return vals, idxs


def _route_select(sv1, si1, sv2, si2):
    w = sv1[0].shape[1]
    v2 = jnp.concatenate(sv2, axis=0)
    i2 = jnp.concatenate(si2, axis=0)
    cv, ca, cb = [], [], []
    for k1, n in enumerate(_ROW_LEN):
        cv.append(sv1[k1] + v2[0:n])
        ca.append(jnp.broadcast_to(si1[k1], (n, w)))
        cb.append(i2[0:n])
    pad = _NCAND_PAD - _NCAND
    if pad:
        cv.append(jnp.full((pad, w), -jnp.inf, F32))
        ca.append(jnp.zeros((pad, w), jnp.int32))
        cb.append(jnp.zeros((pad, w), jnp.int32))
    cand = jnp.concatenate(cv, axis=0)
    ca = jnp.concatenate(ca, axis=0)
    cb = jnp.concatenate(cb, axis=0)
    iota = lax.broadcasted_iota(jnp.int32, cand.shape, 0)
    fv, fa, fb = [], [], []
    for _ in range(PEER_TOPK):
        m = jnp.max(cand, axis=0, keepdims=True)
        idx = jnp.min(jnp.where(cand == m, iota, _NCAND_PAD), axis=0, keepdims=True)
        sel = iota == idx
        fv.append(m)
        fa.append(jnp.sum(jnp.where(sel, ca, 0), axis=0, keepdims=True))
        fb.append(jnp.sum(jnp.where(sel, cb, 0), axis=0, keepdims=True))
        cand = jnp.where(sel, -jnp.inf, cand)
    fv = jnp.concatenate(fv, axis=0)
    e = jnp.exp(fv - fv[0:1])
    g = e / jnp.sum(e, axis=0, keepdims=True)
    return jnp.concatenate(fa, axis=0), jnp.concatenate(fb, axis=0), g


def _route_kernel(x_ref, wq_ref, sk_ref, a_ref, b_ref, g_ref, at_ref, bt_ref, gt_ref, *, tm):
    xb = x_ref[...].astype(BF16)

    def head(hh, carry):
        qh = _dot(xb, wq_ref[hh]).astype(BF16)
        row = pl.multiple_of(hh * PEER_TOPK, PEER_TOPK)
        for ch in range(tm // TILE):
            qc = qh[ch * TILE:(ch + 1) * TILE]
            tops = []
            for c in range(2):
                st = _dot_nt(sk_ref[hh, c], qc[:, c * PEER_KEYS:(c + 1) * PEER_KEYS])
                tops.append(_topk_rows(st, PEER_TOPK))
            a, b, g = _route_select(tops[0][0], tops[0][1], tops[1][0], tops[1][1])
            cols = slice(ch * TILE, (ch + 1) * TILE)
            at_ref[pl.ds(row, PEER_TOPK), cols] = a
            bt_ref[pl.ds(row, PEER_TOPK), cols] = b
            gt_ref[pl.ds(row, PEER_TOPK), cols] = g
        return carry

    lax.fori_loop(0, PEER_HEADS, head, 0)
    for ch in range(tm // TILE):
        rows = slice(ch * TILE, (ch + 1) * TILE)
        a_ref[rows, :] = at_ref[:, rows].T
        b_ref[rows, :] = bt_ref[:, rows].T
        g_ref[rows, :] = gt_ref[:, rows].T


def _route(x, wq3, sk, tm):
    t, d = x.shape
    nsel = PEER_HEADS * PEER_TOPK
    spec = pl.BlockSpec((tm, nsel), lambda i: (i, 0))
    return pl.pallas_call(
        functools.partial(_route_kernel, tm=tm),
        grid=(t // tm,),
        in_specs=[pl.BlockSpec((tm, d), lambda i: (i, 0)),
                  pl.BlockSpec(wq3.shape, lambda i: (0, 0, 0)),
                  pl.BlockSpec(sk.shape, lambda i: (0, 0, 0, 0))],
        out_specs=[spec, spec, spec],
        out_shape=[jax.ShapeDtypeStruct((t, nsel), jnp.int32),
                   jax.ShapeDtypeStruct((t, nsel), jnp.int32),
                   jax.ShapeDtypeStruct((t, nsel), F32)],
        scratch_shapes=[pltpu.VMEM((nsel, tm), jnp.int32), pltpu.VMEM((nsel, tm), jnp.int32),
                        pltpu.VMEM((nsel, tm), F32)],
        compiler_params=_params("parallel"),
        name="peer_route",
    )(x, wq3, sk)


def _wg_kernel(a_ref, b_ref, g_ref, o_ref, *, tb):
    iota = lax.broadcasted_iota(jnp.int32, (PEER_KEYS, PEER_HEADS * PEER_TOPK), 0)

    def body(t, carry):
        ra = a_ref[pl.ds(t, 1), :]
        rb = b_ref[pl.ds(t, 1), :]
        rg = g_ref[pl.ds(t, 1), :]
        pa = jnp.where(iota == ra, 1.0, 0.0).astype(BF16)
        pb = jnp.where(iota == rb, rg, 0.0).astype(BF16)
        o_ref[t] = _dot_nt(pa, pb).astype(BF16)
        return carry

    lax.fori_loop(0, tb, body, 0)


def _wg(a, b, g, tb):
    t, nsel = a.shape
    spec = pl.BlockSpec((tb, nsel), lambda i: (i, 0))
    return pl.pallas_call(
        functools.partial(_wg_kernel, tb=tb),
        grid=(t // tb,),
        in_specs=[spec, spec, spec],
        out_specs=pl.BlockSpec((tb, PEER_KEYS, PEER_KEYS), lambda i: (i, 0, 0)),
        out_shape=jax.ShapeDtypeStruct((t, PEER_KEYS, PEER_KEYS), BF16),
        compiler_params=_params("parallel"),
        name="peer_wg",
    )(a, b, g)


def _experts_kernel(x_ref, wg_ref, u_ref, v_ref, g_ref, b_ref, o_ref, xb_ref, acc_ref, *, alpha):
    e = pl.program_id(1)

    @pl.when(e == 0)
    def _():
        xb_ref[...] = x_ref[...].astype(BF16)
        acc_ref[...] = jnp.zeros_like(acc_ref)

    act = jax.nn.gelu(_dot_nt(xb_ref[...], u_ref[...]))
    p = (act * wg_ref[...].astype(F32)).astype(BF16)
    acc_ref[...] += _dot(p, v_ref[...])

    @pl.when(e == pl.num_programs(1) - 1)
    def _():
        o_ref[...] = _ln(alpha * x_ref[...] + acc_ref[...], g_ref[...], b_ref[...])


def _experts(x, wg, u, v, g, b, alpha, tm, eb):
    t, d = x.shape
    ne = u.shape[0]
    return pl.pallas_call(
        functools.partial(_experts_kernel, alpha=alpha),
        grid=(t // tm, ne // eb),
        in_specs=[pl.BlockSpec((tm, d), lambda i, e: (i, 0)),
                  pl.BlockSpec((tm, eb), lambda i, e: (i, e)),
                  pl.BlockSpec((eb, d), lambda i, e: (e, 0)),
                  pl.BlockSpec((eb, d), lambda i, e: (e, 0)),
                  pl.BlockSpec((1, d), lambda i, e: (0, 0)),
                  pl.BlockSpec((1, d), lambda i, e: (0, 0))],
        out_specs=pl.BlockSpec((tm, d), lambda i, e: (i, 0)),
        out_shape=jax.ShapeDtypeStruct((t, d), F32),
        scratch_shapes=[pltpu.VMEM((tm, d), BF16), pltpu.VMEM((tm, d), F32)],
        compiler_params=_params("parallel", "arbitrary"),
        name="peer_experts",
    )(x, wg, u, v, g.reshape(1, d), b.reshape(1, d))


def _layer(x, nseq, seqlen, pos0, mem_k, mem_v, s0, lw, alpha):
    t, d = x.shape
    tm = min(512, t)
    h = _mm(x, lw["w_in"], tm, 1024)
    mix, s_new, van = _mixer(h, nseq, seqlen, pos0, s0, lw["w_s"], lw["b_s"], lw["gate_ln_g"],
                             lw["gate_ln_b"], lw["ret_gn_g"], lw["ret_gn_b"])
    x = _mm_res_ln(mix, lw["w_o"], x, lw["ln1_g"], lw["ln1_b"], alpha, tm)
    qc = _mm(x, lw["ca_wq"], tm, 1024)
    ca = _attn(qc, mem_k, mem_v, nseq, seqlen)
    x = _mm_res_ln(ca, lw["ca_wo"], x, lw["ln2_g"], lw["ln2_b"], alpha, tm)
    a, b, g = _route(x, lw["peer_wq"], lw["peer_subkeys"], min(256, t))
    wg = _wg(a, b, g, 64).reshape(t, PEER_KEYS * PEER_KEYS)
    x = _experts(x, wg, lw["peer_u"], lw["peer_v"], lw["ln3_g"], lw["ln3_b"], alpha, tm, 1024)
    return x, s_new, van


def kernel(x_prompt, x_sample, mem_prompt, cache_mem_k, cache_mem_v, state_ret, w_in, w_s, b_s, gate_ln_g, gate_ln_b, ret_gn_g, ret_gn_b, w_o, ln1_g, ln1_b, ca_wq, ca_wk, ca_wv, ca_wo, ln2_g, ln2_b, peer_wq, peer_subkeys, peer_u, peer_v, ln3_g, ln3_b):
    depth = w_in.shape[0]
    bp, lp, d = x_prompt.shape
    bs, ls, _ = x_sample.shape
    alpha = (2 * depth) ** 0.25
    yp = x_prompt.reshape(bp * lp, d)
    ys = x_sample.reshape(bs * ls, d)
    mem2d = mem_prompt.reshape(bp * N_MEM, d)
    past_len = 16384
    mk_l, mv_l, sp_l, ss_l, gv_l = [], [], [], [], []
    for l in range(depth):
        lw = {
            "w_in": w_in[l].astype(BF16), "w_s": w_s[l], "b_s": b_s[l],
            "gate_ln_g": gate_ln_g[l], "gate_ln_b": gate_ln_b[l],
            "ret_gn_g": ret_gn_g[l], "ret_gn_b": ret_gn_b[l],
            "w_o": w_o[l].astype(BF16), "ln1_g": ln1_g[l], "ln1_b": ln1_b[l],
            "ca_wq": ca_wq[l].astype(BF16), "ca_wo": ca_wo[l].astype(BF16),
            "ln2_g": ln2_g[l], "ln2_b": ln2_b[l],
            "peer_wq": peer_wq[l].astype(BF16).reshape(d, PEER_HEADS, 2 * PEER_KEYS).transpose(1, 0, 2),
            "peer_subkeys": peer_subkeys[l].astype(BF16),
            "peer_u": peer_u[l].astype(BF16), "peer_v": peer_v[l].astype(BF16),
            "ln3_g": ln3_g[l], "ln3_b": ln3_b[l],
        }
        mem_k = _mm(mem2d, ca_wk[l].astype(BF16), 512, 1024)
        mem_v = _mm(mem2d, ca_wv[l].astype(BF16), 512, 1024)
        yp, sp, _ = _layer(yp, bp, lp, 0, mem_k.reshape(bp, N_MEM, d), mem_v.reshape(bp, N_MEM, d),
                           None, lw, alpha)
        ys, ss, gv = _layer(ys, bs, ls, past_len, cache_mem_k[l].reshape(bs, N_MEM, d),
                            cache_mem_v[l].reshape(bs, N_MEM, d), state_ret[l], lw, alpha)
        mk_l.append(mem_k.reshape(bp, N_MEM, CA_HEADS, CA_DH))
        mv_l.append(mem_v.reshape(bp, N_MEM, CA_HEADS, CA_DH))
        sp_l.append(sp)
        ss_l.append(ss)
        gv_l.append(gv.reshape(bs, ls, A_GROUPS, A_CH))
    return (yp.reshape(bp, lp, d), ys.reshape(bs, ls, d), jnp.stack(mk_l), jnp.stack(mv_l),
            jnp.stack(sp_l), jnp.stack(ss_l), jnp.stack(gv_l))
```

```python
import functools

import jax
import jax.numpy as jnp
from jax import lax
from jax.experimental import pallas as pl
from jax.experimental.pallas import tpu as pltpu

F32 = jnp.float32
BF16 = jnp.bfloat16

A_GROUPS = 4
A_CH = 128
CHUNK = 128
R_HEADS = 4
R_DK = 128
ROPE_BASE = 10000.0
N_MEM = 256
CA_HEADS = 4
CA_DH = 256
PEER_HEADS = 8
PEER_KEYS = 128
PEER_TOPK = 16
LN_EPS = 1e-5
TILE = 128
VMEM_LIMIT = 48 * 1024 * 1024


def _params(*sem):
    return pltpu.CompilerParams(dimension_semantics=sem, vmem_limit_bytes=VMEM_LIMIT)


def _ln(x, g, b):
    mu = jnp.mean(x, axis=-1, keepdims=True)
    xc = x - mu
    var = jnp.mean(xc * xc, axis=-1, keepdims=True)
    return xc * lax.rsqrt(var + LN_EPS) * g + b


def _dot(a, b):
    return jnp.dot(a, b, preferred_element_type=F32)


def _dot_nt(a, b):
    return lax.dot_general(a, b, (((1,), (1,)), ((), ())), preferred_element_type=F32)


def _mm_kernel(x_ref, w_ref, o_ref):
    o_ref[...] = _dot(x_ref[...].astype(BF16), w_ref[...])


def _mm(x, w, tm, tn):
    t, k = x.shape
    n = w.shape[1]
    return pl.pallas_call(
        _mm_kernel,
        grid=(t // tm, n // tn),
        in_specs=[pl.BlockSpec((tm, k), lambda i, j: (i, 0)),
                  pl.BlockSpec((k, tn), lambda i, j: (0, j))],
        out_specs=pl.BlockSpec((tm, tn), lambda i, j: (i, j)),
        out_shape=jax.ShapeDtypeStruct((t, n), F32),
        compiler_params=_params("parallel", "arbitrary"),
        name="mm",
    )(x, w)


def _mm_res_ln_kernel(a_ref, w_ref, x_ref, g_ref, b_ref, o_ref, *, alpha):
    y = _dot(a_ref[...], w_ref[...])
    o_ref[...] = _ln(alpha * x_ref[...] + y, g_ref[...], b_ref[...])


def _mm_res_ln(a, w, x, g, b, alpha, tm):
    t, k = a.shape
    d = w.shape[1]
    return pl.pallas_call(
        functools.partial(_mm_res_ln_kernel, alpha=alpha),
        grid=(t // tm,),
        in_specs=[pl.BlockSpec((tm, k), lambda i: (i, 0)),
                  pl.BlockSpec((k, d), lambda i: (0, 0)),
                  pl.BlockSpec((tm, d), lambda i: (i, 0)),
                  pl.BlockSpec((1, d), lambda i: (0, 0)),
                  pl.BlockSpec((1, d), lambda i: (0, 0))],
        out_specs=pl.BlockSpec((tm, d), lambda i: (i, 0)),
        out_shape=jax.ShapeDtypeStruct((t, d), F32),
        compiler_params=_params("parallel"),
        name="mm_res_ln",
    )(a, w, x, g.reshape(1, d), b.reshape(1, d))


def _mixer_consts(lc, pos):
    half = R_DK // 2
    inv = 1.0 / (ROPE_BASE ** (jnp.arange(half, dtype=F32) / half))
    ang = pos.astype(F32)[:, None] * inv[None, :]
    cos, sin = jnp.cos(ang), jnp.sin(ang)
    rot_c = jnp.concatenate([cos, cos], axis=-1)
    rot_s = jnp.concatenate([-sin, sin], axis=-1)
    lg = jnp.log(1.0 - 2.0 ** (-5.0 - jnp.arange(R_HEADS, dtype=F32)))
    r = jnp.arange(TILE)
    i = (r % lc).astype(F32)
    same = (r[:, None] // lc) == (r[None, :] // lc)
    diff = i[:, None] - i[None, :]
    dm = jnp.where(same[None] & (diff[None] >= 0),
                   jnp.exp(jnp.maximum(diff, 0.0)[None] * lg[:, None, None]), 0.0)
    ones = jnp.ones((1, 1, TILE), F32)
    qd = jnp.exp((i + 1.0)[None, :] * lg[:, None])[:, :, None] * ones
    kd = jnp.exp((lc - 1.0 - i)[None, :] * lg[:, None])[:, :, None] * ones
    cd = jnp.exp(lc * lg)[:, None, None] * jnp.ones((1, 8, TILE), F32)
    return rot_c, rot_s, dm, qd, kd, cd


def _rot(x, c, s):
    return x * c + pltpu.roll(x, R_DK // 2, 1) * s


def _gate_part(h_ref, ws_ref, bsb_ref, glg_ref, glb_ref, mix_ref, van_ref, lc):
    r = lax.broadcasted_iota(jnp.int32, (TILE, TILE), 0)
    c = lax.broadcasted_iota(jnp.int32, (TILE, TILE), 1)
    mask = (c <= r) & ((r // lc) == (c // lc))
    half = A_GROUPS * A_CH
    for g in range(A_GROUPS):
        sl = slice(g * A_CH, (g + 1) * A_CH)
        u = jax.nn.gelu(h_ref[:, sl])
        v = jax.nn.gelu(h_ref[:, half + g * A_CH: half + (g + 1) * A_CH])
        vn = _ln(v, glg_ref[g:g + 1, :], glb_ref[g:g + 1, :])
        w = jnp.where(mask, ws_ref[g], 0.0).astype(BF16)
        mixed = _dot(w, vn.astype(BF16)) + bsb_ref[g]
        mix_ref[:, sl] = (u * mixed).astype(BF16)
        if van_ref is not None:
            van_ref[:, sl] = vn


def _ret_head_inputs(h_ref, hd, rc, rs):
    base = 2 * A_GROUPS * A_CH
    w = R_HEADS * R_DK
    sl = lambda j: slice(base + j * w + hd * R_DK, base + j * w + (hd + 1) * R_DK)
    q = _rot(h_ref[:, sl(0)], rc, rs)
    k = _rot(h_ref[:, sl(1)], rc, rs) * (R_DK ** -0.5)
    return q, k, h_ref[:, sl(2)], h_ref[:, sl(3)]


def _ret_finish(o, gr, gng_ref, gnb_ref, mix_ref, hd):
    on = _ln(o, gng_ref[hd:hd + 1, :], gnb_ref[hd:hd + 1, :])
    off = A_GROUPS * A_CH + hd * R_DK
    mix_ref[:, off:off + R_DK] = (jax.nn.silu(gr) * on).astype(BF16)


def _mixer_prompt_kernel(h_ref, rc_ref, rs_ref, ws_ref, bsb_ref, glg_ref, glb_ref, dm_ref, qd_ref,
                         kd_ref, cd_ref, gng_ref, gnb_ref, mix_ref, s_ref):
    @pl.when(pl.program_id(1) == 0)
    def _():
        s_ref[...] = jnp.zeros_like(s_ref)

    _gate_part(h_ref, ws_ref, bsb_ref, glg_ref, glb_ref, mix_ref, None, CHUNK)
    rc, rs = rc_ref[...], rs_ref[...]
    for hd in range(R_HEADS):
        q, k, v, gr = _ret_head_inputs(h_ref, hd, rc, rs)
        qb, kb, vb = q.astype(BF16), k.astype(BF16), v.astype(BF16)
        inner = _dot_nt(qb, kb) * dm_ref[hd]
        s = s_ref[0, hd]
        o = _dot(inner.astype(BF16), vb) + _dot(qb, s.astype(BF16)) * qd_ref[hd]
        kt = (k * kd_ref[hd]).T.astype(BF16)
        s_ref[0, hd] = s * cd_ref[hd, 0:1, :] + _dot(kt, vb)
        _ret_finish(o, gr, gng_ref, gnb_ref, mix_ref, hd)


def _mixer_sample_kernel(h_ref, rc_ref, rs_ref, ws_ref, bsb_ref, glg_ref, glb_ref, dm_ref, qd_ref,
                         kd_ref, cd_ref, gng_ref, gnb_ref, s0_ref, mix_ref, s_ref, van_ref, *, lc):
    nb = TILE // lc
    _gate_part(h_ref, ws_ref, bsb_ref, glg_ref, glb_ref, mix_ref, van_ref, lc)
    rc, rs = rc_ref[...], rs_ref[...]
    rb = lax.broadcasted_iota(jnp.int32, (TILE, R_DK), 0) // lc
    for hd in range(R_HEADS):
        q, k, v, gr = _ret_head_inputs(h_ref, hd, rc, rs)
        qb, kb, vb = q.astype(BF16), k.astype(BF16), v.astype(BF16)
        inner = _dot_nt(qb, kb) * dm_ref[hd]
        cross = jnp.concatenate(
            [_dot(qb[bi * lc:(bi + 1) * lc], s0_ref[bi, hd].astype(BF16)) for bi in range(nb)], axis=0)
        o = _dot(inner.astype(BF16), vb) + cross * qd_ref[hd]
        kt = (k * kd_ref[hd]).T.astype(BF16)
        cd = cd_ref[hd, 0:1, :]
        for bi in range(nb):
            vm = jnp.where(rb == bi, v, 0.0).astype(BF16)
            s_ref[bi, hd] = s0_ref[bi, hd] * cd + _dot(kt, vm)
        _ret_finish(o, gr, gng_ref, gnb_ref, mix_ref, hd)


def _mixer(h, nseq, seqlen, pos0, s0, w_s, b_s, glg, glb, gng, gnb):
    t, win = h.shape
    dmix = A_GROUPS * A_CH + R_HEADS * R_DK
    if seqlen % CHUNK == 0:
        lc, nchunk = CHUNK, seqlen // CHUNK
        pos = pos0 + jnp.arange(seqlen, dtype=jnp.int32)
    else:
        assert TILE % seqlen == 0 and nseq % (TILE // seqlen) == 0 and s0 is not None
        lc = seqlen
        pos = pos0 + (jnp.arange(TILE, dtype=jnp.int32) % lc)
    rot_c, rot_s, dm, qd, kd, cd = _mixer_consts(lc, pos)
    reps = TILE // lc
    ws_t = jnp.tile(w_s[:, :lc, :lc], (1, reps, reps))
    bsb = jnp.tile(b_s[:, :lc], (1, reps))[:, :, None] * jnp.ones((1, 1, A_CH), F32)
    full = lambda shape: pl.BlockSpec(shape, lambda *_: (0,) * len(shape))
    const_specs = [full((A_GROUPS, TILE, TILE)), full((A_GROUPS, TILE, A_CH)),
                   full((A_GROUPS, A_CH)), full((A_GROUPS, A_CH)),
                   full((R_HEADS, TILE, TILE)), full((R_HEADS, TILE, R_DK)),
                   full((R_HEADS, TILE, R_DK)), full((R_HEADS, 8, R_DK)),
                   full((R_HEADS, R_DK)), full((R_HEADS, R_DK))]
    consts = (ws_t, bsb, glg, glb, dm, qd, kd, cd, gng, gnb)
    if lc == CHUNK:
        mix, s_new = pl.pallas_call(
            _mixer_prompt_kernel,
            grid=(nseq, nchunk),
            in_specs=[pl.BlockSpec((TILE, win), lambda b, n: (b * nchunk + n, 0)),
                      pl.BlockSpec((TILE, R_DK), lambda b, n: (n, 0)),
                      pl.BlockSpec((TILE, R_DK), lambda b, n: (n, 0))] + const_specs,
            out_specs=[pl.BlockSpec((TILE, dmix), lambda b, n: (b * nchunk + n, 0)),
                       pl.BlockSpec((1, R_HEADS, R_DK, R_DK), lambda b, n: (b, 0, 0, 0))],
            out_shape=[jax.ShapeDtypeStruct((t, dmix), BF16),
                       jax.ShapeDtypeStruct((nseq, R_HEADS, R_DK, R_DK), F32)],
            compiler_params=_params("parallel", "arbitrary"),
            name="mixer_prompt",
        )(h, rot_c, rot_s, *consts)
        return mix, s_new, None
    mix, s_new, van = pl.pallas_call(
        functools.partial(_mixer_sample_kernel, lc=lc),
        grid=(t // TILE,),
        in_specs=[pl.BlockSpec((TILE, win), lambda i: (i, 0)),
                  full((TILE, R_DK)), full((TILE, R_DK))] + const_specs
                 + [pl.BlockSpec((reps, R_HEADS, R_DK, R_DK), lambda i: (i, 0, 0, 0))],
        out_specs=[pl.BlockSpec((TILE, dmix), lambda i: (i, 0)),
                   pl.BlockSpec((reps, R_HEADS, R_DK, R_DK), lambda i: (i, 0, 0, 0)),
                   pl.BlockSpec((TILE, A_GROUPS * A_CH), lambda i: (i, 0))],
        out_shape=[jax.ShapeDtypeStruct((t, dmix), BF16),
                   jax.ShapeDtypeStruct((nseq, R_HEADS, R_DK, R_DK), F32),
                   jax.ShapeDtypeStruct((t, A_GROUPS * A_CH), F32)],
        compiler_params=_params("parallel"),
        name="mixer_sample",
    )(h, rot_c, rot_s, *consts, s0)
    return mix, s_new, van


def _attn_rows(q, k, v):
    outs = []
    for hh in range(CA_HEADS):
        sl = slice(hh * CA_DH, (hh + 1) * CA_DH)
        sc = _dot_nt(q[:, sl].astype(BF16), k[:, sl].astype(BF16)) * (CA_DH ** -0.5)
        e = jnp.exp(sc - jnp.max(sc, axis=-1, keepdims=True))
        p = e / jnp.sum(e, axis=-1, keepdims=True)
        outs.append(_dot(p.astype(BF16), v[:, sl].astype(BF16)))
    return outs


def _attn_kernel(q_ref, k_ref, v_ref, o_ref, *, nb, rows):
    for bi in range(nb):
        outs = _attn_rows(q_ref[bi * rows:(bi + 1) * rows, :], k_ref[bi], v_ref[bi])
        for hh, o in enumerate(outs):
            o_ref[bi * rows:(bi + 1) * rows, hh * CA_DH:(hh + 1) * CA_DH] = o.astype(BF16)


def _attn(qc, mem_k, mem_v, nseq, seqlen):
    t, d = qc.shape
    if seqlen >= 512:
        nb, rows = 1, 512
        per_seq = seqlen // rows
        grid = (nseq, per_seq)
        q_map = lambda b, i: (b * per_seq + i, 0)
        kv_map = lambda b, i: (b, 0, 0)
        sem = ("parallel", "arbitrary")
    else:
        nb, rows = 4, seqlen
        grid = (nseq // nb,)
        q_map = lambda i: (i, 0)
        kv_map = lambda i: (i, 0, 0)
        sem = ("parallel",)
    return pl.pallas_call(
        functools.partial(_attn_kernel, nb=nb, rows=rows),
        grid=grid,
        in_specs=[pl.BlockSpec((nb * rows, d), q_map),
                  pl.BlockSpec((nb, N_MEM, d), kv_map),
                  pl.BlockSpec((nb, N_MEM, d), kv_map)],
        out_specs=pl.BlockSpec((nb * rows, d), q_map),
        out_shape=jax.ShapeDtypeStruct((t, d), BF16),
        compiler_params=_params(*sem),
        name="attn",
    )(qc, mem_k, mem_v)


_ROW_LEN = [PEER_TOPK // (k1 + 1) for k1 in range(PEER_TOPK)]
_NCAND = sum(_ROW_LEN)
_NCAND_PAD = -(-_NCAND // 8) * 8


def _topk_rows(s, k):
    n = s.shape[0]
    iota = lax.broadcasted_iota(jnp.int32, s.shape, 0).astype(F32)
    vals, idxs = [], []
    for _ in range(k):
        m = jnp.max(s, axis=0, keepdims=True)
        idx = jnp.min(jnp.where(s == m, iota, float(n)), axis=0, keepdims=True)
        vals.append(m)
        idxs.append(idx)
        s = jnp.where(iota == idx, -jnp.inf, s)
    return vals, idxs


def _route_select(sv1, si1, sv2, si2):
    w = sv1[0].shape[1]
    v2 = jnp.concatenate(sv2, axis=0)
    i2 = jnp.concatenate(si2, axis=0)
    cv, ca, cb = [], [], []
    for k1, n in enumerate(_ROW_LEN):
        cv.append(sv1[k1] + v2[0:n])
        ca.append(jnp.broadcast_to(si1[k1], (n, w)))
        cb.append(i2[0:n])
    pad = _NCAND_PAD - _NCAND
    if pad:
        cv.append(jnp.full((pad, w), -jnp.inf, F32))
        ca.append(jnp.zeros((pad, w), F32))
        cb.append(jnp.zeros((pad, w), F32))
    cand = jnp.concatenate(cv, axis=0)
    ca = jnp.concatenate(ca, axis=0)
    cb = jnp.concatenate(cb, axis=0)
    iota = lax.broadcasted_iota(jnp.int32, cand.shape, 0).astype(F32)
    fv, fa, fb = [], [], []
    for _ in range(PEER_TOPK):
        m = jnp.max(cand, axis=0, keepdims=True)
        idx = jnp.min(jnp.where(cand == m, iota, float(_NCAND_PAD)), axis=0, keepdims=True)
        sel = iota == idx
        fv.append(m)
        fa.append(jnp.sum(jnp.where(sel, ca, 0.0), axis=0, keepdims=True))
        fb.append(jnp.sum(jnp.where(sel, cb, 0.0), axis=0, keepdims=True))
        cand = jnp.where(sel, -jnp.inf, cand)
    fv = jnp.concatenate(fv, axis=0)
    e = jnp.exp(fv - fv[0:1])
    g = e / jnp.sum(e, axis=0, keepdims=True)
    to_int = lambda rows: jnp.concatenate(rows, axis=0).astype(jnp.int32)
    return to_int(fa), to_int(fb), g


def _route_kernel(x_ref, wq_ref, sk_ref, a_ref, b_ref, g_ref, at_ref, bt_ref, gt_ref, *, tm):
    xb = x_ref[...].astype(BF16)

    def head(hh, carry):
        qh = _dot(xb, wq_ref[hh]).astype(BF16)
        row = pl.multiple_of(hh * PEER_TOPK, PEER_TOPK)
        for ch in range(tm // TILE):
            qc = qh[ch * TILE:(ch + 1) * TILE]
            tops = []
            for c in range(2):
                st = _dot_nt(sk_ref[hh, c], qc[:, c * PEER_KEYS:(c + 1) * PEER_KEYS])
                tops.append(_topk_rows(st, PEER_TOPK))
            a, b, g = _route_select(tops[0][0], tops[0][1], tops[1][0], tops[1][1])
            cols = slice(ch * TILE, (ch + 1) * TILE)
            at_ref[pl.ds(row, PEER_TOPK), cols] = a
            bt_ref[pl.ds(row, PEER_TOPK), cols] = b
            gt_ref[pl.ds(row, PEER_TOPK), cols] = g
        return carry

    lax.fori_loop(0, PEER_HEADS, head, 0)
    for ch in range(tm // TILE):
        rows = slice(ch * TILE, (ch + 1) * TILE)
        a_ref[rows, :] = at_ref[:, rows].T
        b_ref[rows, :] = bt_ref[:, rows].T
        g_ref[rows, :] = gt_ref[:, rows].T


def _route(x, wq3, sk, tm):
    t, d = x.shape
    nsel = PEER_HEADS * PEER_TOPK
    spec = pl.BlockSpec((tm, nsel), lambda i: (i, 0))
    return pl.pallas_call(
        functools.partial(_route_kernel, tm=tm),
        grid=(t // tm,),
        in_specs=[pl.BlockSpec((tm, d), lambda i: (i, 0)),
                  pl.BlockSpec(wq3.shape, lambda i: (0, 0, 0)),
                  pl.BlockSpec(sk.shape, lambda i: (0, 0, 0, 0))],
        out_specs=[spec, spec, spec],
        out_shape=[jax.ShapeDtypeStruct((t, nsel), jnp.int32),
                   jax.ShapeDtypeStruct((t, nsel), jnp.int32),
                   jax.ShapeDtypeStruct((t, nsel), F32)],
        scratch_shapes=[pltpu.VMEM((nsel, tm), jnp.int32), pltpu.VMEM((nsel, tm), jnp.int32),
                        pltpu.VMEM((nsel, tm), F32)],
        compiler_params=_params("parallel"),
        name="peer_route",
    )(x, wq3, sk)


_WG_PITCH = PEER_KEYS + 8
_WG_GROUP = 16


def _wg_kernel(a_ref, b_ref, g_ref, o_ref, w_ref, *, tb):
    iota = lax.broadcasted_iota(jnp.int32, (PEER_KEYS, PEER_HEADS * PEER_TOPK), 0)
    for g0 in range(0, tb, _WG_GROUP):
        for t in range(g0, g0 + _WG_GROUP):
            pa = jnp.where(iota == a_ref[t:t + 1, :], 1.0, 0.0).astype(BF16)
            pb = jnp.where(iota == b_ref[t:t + 1, :], g_ref[t:t + 1, :], 0.0).astype(BF16)
            w_ref[t * _WG_PITCH:t * _WG_PITCH + PEER_KEYS, :] = _dot_nt(pa, pb)
        for a in range(PEER_KEYS):
            rows = w_ref[pl.ds(g0 * _WG_PITCH + a, _WG_GROUP, stride=_WG_PITCH), :]
            o_ref[g0:g0 + _WG_GROUP, a * PEER_KEYS:(a + 1) * PEER_KEYS] = rows.astype(BF16)


def _wg(a, b, g, tb):
    t, nsel = a.shape
    spec = pl.BlockSpec((tb, nsel), lambda i: (i, 0))
    return pl.pallas_call(
        functools.partial(_wg_kernel, tb=tb),
        grid=(t // tb,),
        in_specs=[spec, spec, spec],
        out_specs=pl.BlockSpec((tb, PEER_KEYS * PEER_KEYS), lambda i: (i, 0)),
        out_shape=jax.ShapeDtypeStruct((t, PEER_KEYS * PEER_KEYS), BF16),
        scratch_shapes=[pltpu.VMEM((tb * _WG_PITCH, PEER_KEYS), F32)],
        compiler_params=_params("parallel"),
        name="peer_wg",
    )(a, b, g)


def _experts_kernel(x_ref, wg_ref, u_ref, v_ref, g_ref, b_ref, o_ref, xb_ref, acc_ref, *, alpha):
    e = pl.program_id(1)

    @pl.when(e == 0)
    def _():
        xb_ref[...] = x_ref[...].astype(BF16)
        acc_ref[...] = jnp.zeros_like(acc_ref)

    act = jax.nn.gelu(_dot_nt(xb_ref[...], u_ref[...]))
    p = (act * wg_ref[...].astype(F32)).astype(BF16)
    acc_ref[...] += _dot(p, v_ref[...])

    @pl.when(e == pl.num_programs(1) - 1)
    def _():
        o_ref[...] = _ln(alpha * x_ref[...] + acc_ref[...], g_ref[...], b_ref[...])


def _experts(x, wg, u, v, g, b, alpha, tm, eb):
    t, d = x.shape
    ne = u.shape[0]
    return pl.pallas_call(
        functools.partial(_experts_kernel, alpha=alpha),
        grid=(t // tm, ne // eb),
        in_specs=[pl.BlockSpec((tm, d), lambda i, e: (i, 0)),
                  pl.BlockSpec((tm, eb), lambda i, e: (i, e)),
                  pl.BlockSpec((eb, d), lambda i, e: (e, 0)),
                  pl.BlockSpec((eb, d), lambda i, e: (e, 0)),
                  pl.BlockSpec((1, d), lambda i, e: (0, 0)),
                  pl.BlockSpec((1, d), lambda i, e: (0, 0))],
        out_specs=pl.BlockSpec((tm, d), lambda i, e: (i, 0)),
        out_shape=jax.ShapeDtypeStruct((t, d), F32),
        scratch_shapes=[pltpu.VMEM((tm, d), BF16), pltpu.VMEM((tm, d), F32)],
        compiler_params=_params("parallel", "arbitrary"),
        name="peer_experts",
    )(x, wg, u, v, g.reshape(1, d), b.reshape(1, d))


def _layer(x, nseq, seqlen, pos0, mem_k, mem_v, s0, lw, alpha):
    t, d = x.shape
    tm = min(512, t)
    h = _mm(x, lw["w_in"], tm, 1024)
    mix, s_new, van = _mixer(h, nseq, seqlen, pos0, s0, lw["w_s"], lw["b_s"], lw["gate_ln_g"],
                             lw["gate_ln_b"], lw["ret_gn_g"], lw["ret_gn_b"])
    x = _mm_res_ln(mix, lw["w_o"], x, lw["ln1_g"], lw["ln1_b"], alpha, tm)
    qc = _mm(x, lw["ca_wq"], tm, 1024)
    ca = _attn(qc, mem_k, mem_v, nseq, seqlen)
    x = _mm_res_ln(ca, lw["ca_wo"], x, lw["ln2_g"], lw["ln2_b"], alpha, tm)
    a, b, g = _route(x, lw["peer_wq"], lw["peer_subkeys"], min(256, t))
    wg = _wg(a, b, g, 64)
    x = _experts(x, wg, lw["peer_u"], lw["peer_v"], lw["ln3_g"], lw["ln3_b"], alpha, tm, 1024)
    return x, s_new, van


def kernel(x_prompt, x_sample, mem_prompt, cache_mem_k, cache_mem_v, state_ret, w_in, w_s, b_s, gate_ln_g, gate_ln_b, ret_gn_g, ret_gn_b, w_o, ln1_g, ln1_b, ca_wq, ca_wk, ca_wv, ca_wo, ln2_g, ln2_b, peer_wq, peer_subkeys, peer_u, peer_v, ln3_g, ln3_b):
    depth = w_in.shape[0]
    bp, lp, d = x_prompt.shape
    bs, ls, _ = x_sample.shape
    alpha = (2 * depth) ** 0.25
    yp = x_prompt.reshape(bp * lp, d)
    ys = x_sample.reshape(bs * ls, d)
    mem2d = mem_prompt.reshape(bp * N_MEM, d)
    past_len = 16384
    mk_l, mv_l, sp_l, ss_l, gv_l = [], [], [], [], []
    for l in range(depth):
        lw = {
            "w_in": w_in[l].astype(BF16), "w_s": w_s[l], "b_s": b_s[l],
            "gate_ln_g": gate_ln_g[l], "gate_ln_b": gate_ln_b[l],
            "ret_gn_g": ret_gn_g[l], "ret_gn_b": ret_gn_b[l],
            "w_o": w_o[l].astype(BF16), "ln1_g": ln1_g[l], "ln1_b": ln1_b[l],
            "ca_wq": ca_wq[l].astype(BF16), "ca_wo": ca_wo[l].astype(BF16),
            "ln2_g": ln2_g[l], "ln2_b": ln2_b[l],
            "peer_wq": peer_wq[l].astype(BF16).reshape(d, PEER_HEADS, 2 * PEER_KEYS).transpose(1, 0, 2),
            "peer_subkeys": peer_subkeys[l].astype(BF16),
            "peer_u": peer_u[l].astype(BF16), "peer_v": peer_v[l].astype(BF16),
            "ln3_g": ln3_g[l], "ln3_b": ln3_b[l],
        }
        mem_k = _mm(mem2d, ca_wk[l].astype(BF16), 512, 1024)
        mem_v = _mm(mem2d, ca_wv[l].astype(BF16), 512, 1024)
        yp, sp, _ = _layer(yp, bp, lp, 0, mem_k.reshape(bp, N_MEM, d), mem_v.reshape(bp, N_MEM, d),
                           None, lw, alpha)
        ys, ss, gv = _layer(ys, bs, ls, past_len, cache_mem_k[l].reshape(bs, N_MEM, d),
                            cache_mem_v[l].reshape(bs, N_MEM, d), state_ret[l], lw, alpha)
        mk_l.append(mem_k.reshape(bp, N_MEM, CA_HEADS, CA_DH))
        mv_l.append(mem_v.reshape(bp, N_MEM, CA_HEADS, CA_DH))
        sp_l.append(sp)
        ss_l.append(ss)
        gv_l.append(gv.reshape(bs, ls, A_GROUPS, A_CH))
    return (yp.reshape(bp, lp, d), ys.reshape(bs, ls, d), jnp.stack(mk_l), jnp.stack(mv_l),
            jnp.stack(sp_l), jnp.stack(ss_l), jnp.stack(gv_l))
```

```python
import functools

import jax
import jax.numpy as jnp
from jax import lax
from jax.experimental import pallas as pl
from jax.experimental.pallas import tpu as pltpu

F32 = jnp.float32
BF16 = jnp.bfloat16

A_GROUPS = 4
A_CH = 128
CHUNK = 128
R_HEADS = 4
R_DK = 128
ROPE_BASE = 10000.0
N_MEM = 256
CA_HEADS = 4
CA_DH = 256
PEER_HEADS = 8
PEER_KEYS = 128
PEER_TOPK = 16
LN_EPS = 1e-5
PAST_LEN = 16384
TILE = 128
VMEM_LIMIT = 48 * 1024 * 1024


def _params(*sem):
    return pltpu.CompilerParams(dimension_semantics=sem, vmem_limit_bytes=VMEM_LIMIT)


def _ln(x, g, b):
    mu = jnp.mean(x, axis=-1, keepdims=True)
    xc = x - mu
    var = jnp.mean(xc * xc, axis=-1, keepdims=True)
    return xc * lax.rsqrt(var + LN_EPS) * g + b


def _dot(a, b):
    return jnp.dot(a, b, preferred_element_type=F32)


def _dot_nt(a, b):
    return lax.dot_general(a, b, (((1,), (1,)), ((), ())), preferred_element_type=F32)


def _mm_kernel(x_ref, w_ref, o_ref):
    o_ref[...] = _dot(x_ref[...].astype(BF16), w_ref[...])


def _mm(x, w, tm, tn):
    t, k = x.shape
    n = w.shape[1]
    return pl.pallas_call(
        _mm_kernel,
        grid=(t // tm, n // tn),
        in_specs=[pl.BlockSpec((tm, k), lambda i, j: (i, 0)),
                  pl.BlockSpec((k, tn), lambda i, j: (0, j))],
        out_specs=pl.BlockSpec((tm, tn), lambda i, j: (i, j)),
        out_shape=jax.ShapeDtypeStruct((t, n), F32),
        compiler_params=_params("parallel", "arbitrary"),
        name="mm",
    )(x, w)


def _mm_res_ln_kernel(a_ref, w_ref, x_ref, g_ref, b_ref, o_ref, *, alpha):
    y = _dot(a_ref[...], w_ref[...])
    o_ref[...] = _ln(alpha * x_ref[...] + y, g_ref[...], b_ref[...])


def _mm_res_ln(a, w, x, g, b, alpha, tm):
    t, k = a.shape
    d = w.shape[1]
    return pl.pallas_call(
        functools.partial(_mm_res_ln_kernel, alpha=alpha),
        grid=(t // tm,),
        in_specs=[pl.BlockSpec((tm, k), lambda i: (i, 0)),
                  pl.BlockSpec((k, d), lambda i: (0, 0)),
                  pl.BlockSpec((tm, d), lambda i: (i, 0)),
                  pl.BlockSpec((1, d), lambda i: (0, 0)),
                  pl.BlockSpec((1, d), lambda i: (0, 0))],
        out_specs=pl.BlockSpec((tm, d), lambda i: (i, 0)),
        out_shape=jax.ShapeDtypeStruct((t, d), F32),
        compiler_params=_params("parallel"),
        name="mm_res_ln",
    )(a, w, x, g.reshape(1, d), b.reshape(1, d))


def _mixer_consts(lc, pos):
    half = R_DK // 2
    inv = 1.0 / (ROPE_BASE ** (jnp.arange(half, dtype=F32) / half))
    ang = pos.astype(F32)[:, None] * inv[None, :]
    cos, sin = jnp.cos(ang), jnp.sin(ang)
    rot_c = jnp.concatenate([cos, cos], axis=-1)
    rot_s = jnp.concatenate([-sin, sin], axis=-1)
    lg = jnp.log(1.0 - 2.0 ** (-5.0 - jnp.arange(R_HEADS, dtype=F32)))
    r = jnp.arange(TILE)
    i = (r % lc).astype(F32)
    same = (r[:, None] // lc) == (r[None, :] // lc)
    diff = i[:, None] - i[None, :]
    dm = jnp.where(same[None] & (diff[None] >= 0),
                   jnp.exp(jnp.maximum(diff, 0.0)[None] * lg[:, None, None]), 0.0)
    ones = jnp.ones((1, 1, TILE), F32)
    qd = jnp.exp((i + 1.0)[None, :] * lg[:, None])[:, :, None] * ones
    kd = jnp.exp((lc - 1.0 - i)[None, :] * lg[:, None])[:, :, None] * ones
    cd = jnp.exp(lc * lg)[:, None, None] * jnp.ones((1, 8, TILE), F32)
    return rot_c, rot_s, dm, qd, kd, cd


def _rot(x, c, s):
    return x * c + pltpu.roll(x, R_DK // 2, 1) * s


def _gate_part(h_ref, ws_ref, bsb_ref, glg_ref, glb_ref, mix_ref, van_ref, lc):
    r = lax.broadcasted_iota(jnp.int32, (TILE, TILE), 0)
    c = lax.broadcasted_iota(jnp.int32, (TILE, TILE), 1)
    mask = (c <= r) & ((r // lc) == (c // lc))
    half = A_GROUPS * A_CH
    for g in range(A_GROUPS):
        sl = slice(g * A_CH, (g + 1) * A_CH)
        u = jax.nn.gelu(h_ref[:, sl])
        v = jax.nn.gelu(h_ref[:, half + g * A_CH: half + (g + 1) * A_CH])
        vn = _ln(v, glg_ref[g:g + 1, :], glb_ref[g:g + 1, :])
        w = jnp.where(mask, ws_ref[g], 0.0).astype(BF16)
        mixed = _dot(w, vn.astype(BF16)) + bsb_ref[g]
        mix_ref[:, sl] = (u * mixed).astype(BF16)
        if van_ref is not None:
            van_ref[:, sl] = vn


def _ret_head_inputs(h_ref, hd, rc, rs):
    base = 2 * A_GROUPS * A_CH
    w = R_HEADS * R_DK
    sl = lambda j: slice(base + j * w + hd * R_DK, base + j * w + (hd + 1) * R_DK)
    q = _rot(h_ref[:, sl(0)], rc, rs)
    k = _rot(h_ref[:, sl(1)], rc, rs) * (R_DK ** -0.5)
    return q, k, h_ref[:, sl(2)], h_ref[:, sl(3)]


def _ret_finish(o, gr, gng_ref, gnb_ref, mix_ref, hd):
    on = _ln(o, gng_ref[hd:hd + 1, :], gnb_ref[hd:hd + 1, :])
    off = A_GROUPS * A_CH + hd * R_DK
    mix_ref[:, off:off + R_DK] = (jax.nn.silu(gr) * on).astype(BF16)


def _mixer_prompt_kernel(h_ref, rc_ref, rs_ref, ws_ref, bsb_ref, glg_ref, glb_ref, dm_ref, qd_ref,
                         kd_ref, cd_ref, gng_ref, gnb_ref, mix_ref, s_ref):
    @pl.when(pl.program_id(1) == 0)
    def _():
        s_ref[...] = jnp.zeros_like(s_ref)

    _gate_part(h_ref, ws_ref, bsb_ref, glg_ref, glb_ref, mix_ref, None, CHUNK)
    rc, rs = rc_ref[...], rs_ref[...]
    for hd in range(R_HEADS):
        q, k, v, gr = _ret_head_inputs(h_ref, hd, rc, rs)
        qb, kb, vb = q.astype(BF16), k.astype(BF16), v.astype(BF16)
        inner = _dot_nt(qb, kb) * dm_ref[hd]
        s = s_ref[0, hd]
        o = _dot(inner.astype(BF16), vb) + _dot(qb, s.astype(BF16)) * qd_ref[hd]
        kt = (k * kd_ref[hd]).T.astype(BF16)
        s_ref[0, hd] = s * cd_ref[hd, 0:1, :] + _dot(kt, vb)
        _ret_finish(o, gr, gng_ref, gnb_ref, mix_ref, hd)


def _mixer_sample_kernel(h_ref, rc_ref, rs_ref, ws_ref, bsb_ref, glg_ref, glb_ref, dm_ref, qd_ref,
                         kd_ref, cd_ref, gng_ref, gnb_ref, s0_ref, mix_ref, s_ref, van_ref, *, lc):
    nb = TILE // lc
    _gate_part(h_ref, ws_ref, bsb_ref, glg_ref, glb_ref, mix_ref, van_ref, lc)
    rc, rs = rc_ref[...], rs_ref[...]
    rb = lax.broadcasted_iota(jnp.int32, (TILE, R_DK), 0) // lc
    for hd in range(R_HEADS):
        q, k, v, gr = _ret_head_inputs(h_ref, hd, rc, rs)
        qb, kb, vb = q.astype(BF16), k.astype(BF16), v.astype(BF16)
        inner = _dot_nt(qb, kb) * dm_ref[hd]
        cross = jnp.concatenate(
            [_dot(qb[bi * lc:(bi + 1) * lc], s0_ref[bi, hd].astype(BF16)) for bi in range(nb)], axis=0)
        o = _dot(inner.astype(BF16), vb) + cross * qd_ref[hd]
        kt = (k * kd_ref[hd]).T.astype(BF16)
        cd = cd_ref[hd, 0:1, :]
        for bi in range(nb):
            vm = jnp.where(rb == bi, v, 0.0).astype(BF16)
            s_ref[bi, hd] = s0_ref[bi, hd] * cd + _dot(kt, vm)
        _ret_finish(o, gr, gng_ref, gnb_ref, mix_ref, hd)


def _mixer(h, nseq, seqlen, pos0, s0, w_s, b_s, glg, glb, gng, gnb):
    t, win = h.shape
    dmix = A_GROUPS * A_CH + R_HEADS * R_DK
    if seqlen % CHUNK == 0:
        lc, nchunk = CHUNK, seqlen // CHUNK
        pos = pos0 + jnp.arange(seqlen, dtype=jnp.int32)
    else:
        assert TILE % seqlen == 0 and nseq % (TILE // seqlen) == 0 and s0 is not None
        lc = seqlen
        pos = pos0 + (jnp.arange(TILE, dtype=jnp.int32) % lc)
    rot_c, rot_s, dm, qd, kd, cd = _mixer_consts(lc, pos)
    reps = TILE // lc
    ws_t = jnp.tile(w_s[:, :lc, :lc], (1, reps, reps))
    bsb = jnp.tile(b_s[:, :lc], (1, reps))[:, :, None] * jnp.ones((1, 1, A_CH), F32)
    full = lambda shape: pl.BlockSpec(shape, lambda *_: (0,) * len(shape))
    const_specs = [full((A_GROUPS, TILE, TILE)), full((A_GROUPS, TILE, A_CH)),
                   full((A_GROUPS, A_CH)), full((A_GROUPS, A_CH)),
                   full((R_HEADS, TILE, TILE)), full((R_HEADS, TILE, R_DK)),
                   full((R_HEADS, TILE, R_DK)), full((R_HEADS, 8, R_DK)),
                   full((R_HEADS, R_DK)), full((R_HEADS, R_DK))]
    consts = (ws_t, bsb, glg, glb, dm, qd, kd, cd, gng, gnb)
    if lc == CHUNK:
        mix, s_new = pl.pallas_call(
            _mixer_prompt_kernel,
            grid=(nseq, nchunk),
            in_specs=[pl.BlockSpec((TILE, win), lambda b, n: (b * nchunk + n, 0)),
                      pl.BlockSpec((TILE, R_DK), lambda b, n: (n, 0)),
                      pl.BlockSpec((TILE, R_DK), lambda b, n: (n, 0))] + const_specs,
            out_specs=[pl.BlockSpec((TILE, dmix), lambda b, n: (b * nchunk + n, 0)),
                       pl.BlockSpec((1, R_HEADS, R_DK, R_DK), lambda b, n: (b, 0, 0, 0))],
            out_shape=[jax.ShapeDtypeStruct((t, dmix), BF16),
                       jax.ShapeDtypeStruct((nseq, R_HEADS, R_DK, R_DK), F32)],
            compiler_params=_params("parallel", "arbitrary"),
            name="mixer_prompt",
        )(h, rot_c, rot_s, *consts)
        return mix, s_new, None
    mix, s_new, van = pl.pallas_call(
        functools.partial(_mixer_sample_kernel, lc=lc),
        grid=(t // TILE,),
        in_specs=[pl.BlockSpec((TILE, win), lambda i: (i, 0)),
                  full((TILE, R_DK)), full((TILE, R_DK))] + const_specs
                 + [pl.BlockSpec((reps, R_HEADS, R_DK, R_DK), lambda i: (i, 0, 0, 0))],
        out_specs=[pl.BlockSpec((TILE, dmix), lambda i: (i, 0)),
                   pl.BlockSpec((reps, R_HEADS, R_DK, R_DK), lambda i: (i, 0, 0, 0)),
                   pl.BlockSpec((TILE, A_GROUPS * A_CH), lambda i: (i, 0))],
        out_shape=[jax.ShapeDtypeStruct((t, dmix), BF16),
                   jax.ShapeDtypeStruct((nseq, R_HEADS, R_DK, R_DK), F32),
                   jax.ShapeDtypeStruct((t, A_GROUPS * A_CH), F32)],
        compiler_params=_params("parallel"),
        name="mixer_sample",
    )(h, rot_c, rot_s, *consts, s0)
    return mix, s_new, van


def _attn_kernel(q_ref, k_ref, v_ref, o_ref, *, nb, rows):
    split = len(k_ref.shape) == 4
    for bi in range(nb):
        rs = slice(bi * rows, (bi + 1) * rows)
        for hh in range(CA_HEADS):
            sl = slice(hh * CA_DH, (hh + 1) * CA_DH)
            kh = k_ref[bi, :, hh, :] if split else k_ref[bi, :, sl]
            vh = v_ref[bi, :, hh, :] if split else v_ref[bi, :, sl]
            sc = _dot_nt(q_ref[rs, sl].astype(BF16), kh.astype(BF16)) * (CA_DH ** -0.5)
            e = jnp.exp(sc - jnp.max(sc, axis=-1, keepdims=True))
            p = e / jnp.sum(e, axis=-1, keepdims=True)
            o_ref[rs, sl] = _dot(p.astype(BF16), vh.astype(BF16)).astype(BF16)


def _attn(qc, mem_k, mem_v, nseq, seqlen):
    t, d = qc.shape
    kv_tail = mem_k.shape[1:]
    zeros = (0,) * len(kv_tail)
    if seqlen >= 512:
        nb, rows = 1, 512
        per_seq = seqlen // rows
        grid = (nseq, per_seq)
        q_map = lambda b, i: (b * per_seq + i, 0)
        kv_map = lambda b, i: (b,) + zeros
        sem = ("parallel", "arbitrary")
    else:
        nb, rows = 2, seqlen
        grid = (nseq // nb,)
        q_map = lambda i: (i, 0)
        kv_map = lambda i: (i,) + zeros
        sem = ("parallel",)
    return pl.pallas_call(
        functools.partial(_attn_kernel, nb=nb, rows=rows),
        grid=grid,
        in_specs=[pl.BlockSpec((nb * rows, d), q_map),
                  pl.BlockSpec((nb,) + kv_tail, kv_map),
                  pl.BlockSpec((nb,) + kv_tail, kv_map)],
        out_specs=pl.BlockSpec((nb * rows, d), q_map),
        out_shape=jax.ShapeDtypeStruct((t, d), BF16),
        compiler_params=_params(*sem),
        name="attn",
    )(qc, mem_k, mem_v)


_ROW_LEN = [PEER_TOPK // (k1 + 1) for k1 in range(PEER_TOPK)]
_NCAND = sum(_ROW_LEN)
_NCAND_PAD = -(-_NCAND // 8) * 8


def _topk_rows(s, k):
    n = s.shape[0]
    iota = lax.broadcasted_iota(jnp.int32, s.shape, 0).astype(F32)
    vals, idxs = [], []
    for _ in range(k):
        m = jnp.max(s, axis=0, keepdims=True)
        idx = jnp.min(jnp.where(s == m, iota, float(n)), axis=0, keepdims=True)
        vals.append(m)
        idxs.append(idx)
        s = jnp.where(iota == idx, -jnp.inf, s)
    return vals, idxs


def _route_select(sv1, si1, sv2, si2):
    w = sv1[0].shape[1]
    v2 = jnp.concatenate(sv2, axis=0)
    i2 = jnp.concatenate(si2, axis=0)
    cv, ca, cb = [], [], []
    for k1, n in enumerate(_ROW_LEN):
        cv.append(sv1[k1] + v2[0:n])
        ca.append(jnp.broadcast_to(si1[k1], (n, w)))
        cb.append(i2[0:n])
    pad = _NCAND_PAD - _NCAND
    if pad:
        cv.append(jnp.full((pad, w), -jnp.inf, F32))
        ca.append(jnp.zeros((pad, w), F32))
        cb.append(jnp.zeros((pad, w), F32))
    cand = jnp.concatenate(cv, axis=0)
    ca = jnp.concatenate(ca, axis=0)
    cb = jnp.concatenate(cb, axis=0)
    iota = lax.broadcasted_iota(jnp.int32, cand.shape, 0).astype(F32)
    fv, fa, fb = [], [], []
    for _ in range(PEER_TOPK):
        m = jnp.max(cand, axis=0, keepdims=True)
        idx = jnp.min(jnp.where(cand == m, iota, float(_NCAND_PAD)), axis=0, keepdims=True)
        sel = iota == idx
        fv.append(m)
        fa.append(jnp.sum(jnp.where(sel, ca, 0.0), axis=0, keepdims=True))
        fb.append(jnp.sum(jnp.where(sel, cb, 0.0), axis=0, keepdims=True))
        cand = jnp.where(sel, -jnp.inf, cand)
    fv = jnp.concatenate(fv, axis=0)
    e = jnp.exp(fv - fv[0:1])
    g = e / jnp.sum(e, axis=0, keepdims=True)
    to_int = lambda rows: jnp.concatenate(rows, axis=0).astype(jnp.int32)
    return to_int(fa), to_int(fb), g


_ID_OFF = 128.0
_TIE = 2 * _ID_OFF


def _col_max(x):
    tiles = [x[8 * i:8 * i + 8] for i in range(x.shape[0] // 8)]
    while len(tiles) > 1:
        tiles = [jnp.maximum(tiles[i], tiles[i + 1]) for i in range(0, len(tiles) - 1, 2)] + tiles[len(tiles) & ~1:]
    return jnp.max(tiles[0], axis=0, keepdims=True)


def _topk_unique(scores, ids_lhs, sub):
    zero = jnp.zeros((8, scores[0].shape[1]), F32)
    scores = list(scores)
    out = [([zero, zero], [zero, zero], zero) for _ in scores]
    for k in range(PEER_TOPK):
        here = sub == (k % 8)
        for j, s in enumerate(scores):
            v, r, worst = out[j]
            m = _col_max(s)
            e = s == m
            scores[j] = jnp.where(e, -jnp.inf, s)
            rk = _dot(ids_lhs, jnp.where(e, 1.0, 0.0).astype(BF16))[0:8]
            v[k // 8] = jnp.where(here, m, v[k // 8])
            r[k // 8] = jnp.where(here, rk, r[k // 8])
            out[j] = (v, r, jnp.maximum(worst, rk))
    return out


def _select_unique(v1, r1, v2, r2, ones_lhs, sub):
    w = v1[0].shape[1]
    bc = lambda x, row: jnp.broadcast_to(x[row:row + 1], (8, w))
    a1 = [r1[0] - _ID_OFF, r1[1] - _ID_OFF]
    v10, a10 = bc(v1[0], 0), bc(a1[0], 0)
    cand, ca, cb = [v10 + v2[0], v10 + v2[1]], [a10, a10], [r2[0], r2[1]]
    for k1 in range(1, 8):
        c = bc(v1[0], k1) + v2[0]
        if _ROW_LEN[k1] < 8:
            c = jnp.where(sub < _ROW_LEN[k1], c, -jnp.inf)
        cand.append(c)
        ca.append(bc(a1[0], k1))
        cb.append(r2[0])
    cand.append(v1[1] + bc(v2[0], 0))
    ca.append(a1[1])
    cb.append(bc(r2[0], 0))
    cand = jnp.concatenate(cand, axis=0)
    ca = jnp.concatenate(ca, axis=0)
    cb = jnp.concatenate(cb, axis=0)
    zero = jnp.zeros((8, w), F32)
    fv, fa, fb, worst = [zero, zero], [zero, zero], [zero, zero], zero
    for k in range(PEER_TOPK):
        m = _col_max(cand)
        e = cand == m
        cand = jnp.where(e, -jnp.inf, cand)
        ra = _dot(ones_lhs, jnp.where(e, ca, 0.0).astype(BF16))[0:8]
        rb = _dot(ones_lhs, jnp.where(e, cb, 0.0).astype(BF16))[0:8]
        worst = jnp.maximum(worst, rb)
        here = sub == (k % 8)
        fv[k // 8] = jnp.where(here, m, fv[k // 8])
        fa[k // 8] = jnp.where(here, ra, fa[k // 8])
        fb[k // 8] = jnp.where(here, rb, fb[k // 8])
    fv = jnp.concatenate(fv, axis=0)
    ex = jnp.exp(fv - fv[0:1])
    g = ex / jnp.sum(ex, axis=0, keepdims=True)
    a = jnp.concatenate(fa, axis=0).astype(jnp.int32)
    b = (jnp.concatenate(fb, axis=0) - _ID_OFF).astype(jnp.int32)
    return a, b, g, worst


def _route_kernel(x_ref, wq_ref, sk_ref, a_ref, b_ref, g_ref, s_ref, at_ref, bt_ref, gt_ref, *, tm):
    nch = tm // TILE
    sub = lax.broadcasted_iota(jnp.int32, (8, TILE), 0)
    ids_lhs = (lax.broadcasted_iota(jnp.int32, (16, PEER_KEYS), 1).astype(F32) + _ID_OFF).astype(BF16)
    ones_lhs = jnp.ones((16, 10 * 8), BF16)

    q = _dot(x_ref[...].astype(BF16), wq_ref[...]).astype(BF16)
    for hh in range(PEER_HEADS):
        for ch in range(nch):
            for c in range(2):
                col = (2 * hh + c) * PEER_KEYS
                s_ref[hh, 2 * ch + c] = _dot_nt(sk_ref[hh, c], q[ch * TILE:(ch + 1) * TILE, col:col + PEER_KEYS])

    def store(hh, ch, a, b, g):
        row = pl.multiple_of(hh * PEER_TOPK, PEER_TOPK)
        cols = slice(ch * TILE, (ch + 1) * TILE)
        at_ref[pl.ds(row, PEER_TOPK), cols] = a
        bt_ref[pl.ds(row, PEER_TOPK), cols] = b
        gt_ref[pl.ds(row, PEER_TOPK), cols] = g

    def head_unique(hh, worst):
        for ch in range(nch):
            (v1, r1, w1), (v2, r2, w2) = _topk_unique([s_ref[hh, 2 * ch], s_ref[hh, 2 * ch + 1]], ids_lhs, sub)
            a, b, g, w3 = _select_unique(v1, r1, v2, r2, ones_lhs, sub)
            store(hh, ch, a, b, g)
            worst = jnp.maximum(jnp.maximum(worst, w1), jnp.maximum(w2, w3))
        return worst

    worst = lax.fori_loop(0, PEER_HEADS, head_unique, jnp.zeros((8, TILE), F32))

    @pl.when(jnp.max(worst) >= _TIE)
    def _():
        def head_exact(hh, carry):
            for ch in range(nch):
                tops = [_topk_rows(s_ref[hh, 2 * ch + c], PEER_TOPK) for c in range(2)]
                store(hh, ch, *_route_select(tops[0][0], tops[0][1], tops[1][0], tops[1][1]))
            return carry

        lax.fori_loop(0, PEER_HEADS, head_exact, 0)

    for ch in range(nch):
        rows = slice(ch * TILE, (ch + 1) * TILE)
        a_ref[rows, :] = at_ref[:, rows].T
        b_ref[rows, :] = bt_ref[:, rows].T
        g_ref[rows, :] = gt_ref[:, rows].T


def _route(x, wq, sk, tm):
    t, d = x.shape
    nsel = PEER_HEADS * PEER_TOPK
    spec = pl.BlockSpec((tm, nsel), lambda i: (i, 0))
    return pl.pallas_call(
        functools.partial(_route_kernel, tm=tm),
        grid=(t // tm,),
        in_specs=[pl.BlockSpec((tm, d), lambda i: (i, 0)),
                  pl.BlockSpec(wq.shape, lambda i: (0, 0)),
                  pl.BlockSpec(sk.shape, lambda i: (0, 0, 0, 0))],
        out_specs=[spec, spec, spec],
        out_shape=[jax.ShapeDtypeStruct((t, nsel), jnp.int32),
                   jax.ShapeDtypeStruct((t, nsel), jnp.int32),
                   jax.ShapeDtypeStruct((t, nsel), F32)],
        scratch_shapes=[pltpu.VMEM((PEER_HEADS, 2 * (tm // TILE), PEER_KEYS, TILE), F32),
                        pltpu.VMEM((nsel, tm), jnp.int32), pltpu.VMEM((nsel, tm), jnp.int32),
                        pltpu.VMEM((nsel, tm), F32)],
        compiler_params=_params("parallel"),
        name="peer_route",
    )(x, wq, sk)


_WG_PITCH = PEER_KEYS + 8
_WG_GROUP = 16


def _wg_kernel(a_ref, b_ref, g_ref, o_ref, w_ref, *, tb):
    iota = lax.broadcasted_iota(jnp.int32, (PEER_KEYS, PEER_HEADS * PEER_TOPK), 0)
    for g0 in range(0, tb, _WG_GROUP):
        for t in range(g0, g0 + _WG_GROUP):
            pa = jnp.where(iota == a_ref[t:t + 1, :], 1.0, 0.0).astype(BF16)
            pb = jnp.where(iota == b_ref[t:t + 1, :], g_ref[t:t + 1, :], 0.0).astype(BF16)
            w_ref[t * _WG_PITCH:t * _WG_PITCH + PEER_KEYS, :] = _dot_nt(pa, pb)
        for a in range(PEER_KEYS):
            rows = w_ref[pl.ds(g0 * _WG_PITCH + a, _WG_GROUP, stride=_WG_PITCH), :]
            o_ref[g0:g0 + _WG_GROUP, a * PEER_KEYS:(a + 1) * PEER_KEYS] = rows.astype(BF16)


def _wg(a, b, g, tb):
    t, nsel = a.shape
    spec = pl.BlockSpec((tb, nsel), lambda i: (i, 0))
    return pl.pallas_call(
        functools.partial(_wg_kernel, tb=tb),
        grid=(t // tb,),
        in_specs=[spec, spec, spec],
        out_specs=pl.BlockSpec((tb, PEER_KEYS * PEER_KEYS), lambda i: (i, 0)),
        out_shape=jax.ShapeDtypeStruct((t, PEER_KEYS * PEER_KEYS), BF16),
        scratch_shapes=[pltpu.VMEM((tb * _WG_PITCH, PEER_KEYS), F32)],
        compiler_params=_params("parallel"),
        name="peer_wg",
    )(a, b, g)


def _experts_kernel(x_ref, wg_ref, u_ref, v_ref, g_ref, b_ref, o_ref, xb_ref, acc_ref, *, alpha):
    e = pl.program_id(1)

    @pl.when(e == 0)
    def _():
        xb_ref[...] = x_ref[...].astype(BF16)
        acc_ref[...] = jnp.zeros_like(acc_ref)

    act = jax.nn.gelu(_dot_nt(xb_ref[...], u_ref[...]))
    p = (act * wg_ref[...].astype(F32)).astype(BF16)
    acc_ref[...] += _dot(p, v_ref[...])

    @pl.when(e == pl.num_programs(1) - 1)
    def _():
        o_ref[...] = _ln(alpha * x_ref[...] + acc_ref[...], g_ref[...], b_ref[...])


def _experts(x, wg, u, v, g, b, alpha, tm, eb):
    t, d = x.shape
    ne = u.shape[0]
    return pl.pallas_call(
        functools.partial(_experts_kernel, alpha=alpha),
        grid=(t // tm, ne // eb),
        in_specs=[pl.BlockSpec((tm, d), lambda i, e: (i, 0)),
                  pl.BlockSpec((tm, eb), lambda i, e: (i, e)),
                  pl.BlockSpec((eb, d), lambda i, e: (e, 0)),
                  pl.BlockSpec((eb, d), lambda i, e: (e, 0)),
                  pl.BlockSpec((1, d), lambda i, e: (0, 0)),
                  pl.BlockSpec((1, d), lambda i, e: (0, 0))],
        out_specs=pl.BlockSpec((tm, d), lambda i, e: (i, 0)),
        out_shape=jax.ShapeDtypeStruct((t, d), F32),
        scratch_shapes=[pltpu.VMEM((tm, d), BF16), pltpu.VMEM((tm, d), F32)],
        compiler_params=_params("parallel", "arbitrary"),
        name="peer_experts",
    )(x, wg, u, v, g.reshape(1, d), b.reshape(1, d))


def _layer(x, nseq, seqlen, pos0, mem_k, mem_v, s0, lw, alpha):
    t, d = x.shape
    tm = min(512, t)
    h = _mm(x, lw["w_in"], tm, 1024)
    mix, s_new, van = _mixer(h, nseq, seqlen, pos0, s0, lw["w_s"], lw["b_s"], lw["gate_ln_g"],
                             lw["gate_ln_b"], lw["ret_gn_g"], lw["ret_gn_b"])
    x = _mm_res_ln(mix, lw["w_o"], x, lw["ln1_g"], lw["ln1_b"], alpha, tm)
    qc = _mm(x, lw["ca_wq"], tm, 1024)
    ca = _attn(qc, mem_k, mem_v, nseq, seqlen)
    x = _mm_res_ln(ca, lw["ca_wo"], x, lw["ln2_g"], lw["ln2_b"], alpha, tm)
    a, b, g = _route(x, lw["peer_wq"], lw["peer_subkeys"], min(512, t))
    wg = _wg(a, b, g, 64)
    x = _experts(x, wg, lw["peer_u"], lw["peer_v"], lw["ln3_g"], lw["ln3_b"], alpha, tm, 1024)
    return x, s_new, van


def kernel(x_prompt, x_sample, mem_prompt, cache_mem_k, cache_mem_v, state_ret, w_in, w_s, b_s, gate_ln_g, gate_ln_b, ret_gn_g, ret_gn_b, w_o, ln1_g, ln1_b, ca_wq, ca_wk, ca_wv, ca_wo, ln2_g, ln2_b, peer_wq, peer_subkeys, peer_u, peer_v, ln3_g, ln3_b):
    depth = w_in.shape[0]
    bp, lp, d = x_prompt.shape
    bs, ls, _ = x_sample.shape
    alpha = (2 * depth) ** 0.25
    yp = x_prompt.reshape(bp * lp, d)
    ys = x_sample.reshape(bs * ls, d)
    mem2d = mem_prompt.reshape(bp * N_MEM, d)
    past_len = PAST_LEN
    mk_l, mv_l, sp_l, ss_l, gv_l = [], [], [], [], []
    for l in range(depth):
        lw = {
            "w_in": w_in[l].astype(BF16), "w_s": w_s[l], "b_s": b_s[l],
            "gate_ln_g": gate_ln_g[l], "gate_ln_b": gate_ln_b[l],
            "ret_gn_g": ret_gn_g[l], "ret_gn_b": ret_gn_b[l],
            "w_o": w_o[l].astype(BF16), "ln1_g": ln1_g[l], "ln1_b": ln1_b[l],
            "ca_wq": ca_wq[l].astype(BF16), "ca_wo": ca_wo[l].astype(BF16),
            "ln2_g": ln2_g[l], "ln2_b": ln2_b[l],
            "peer_wq": peer_wq[l].astype(BF16),
            "peer_subkeys": peer_subkeys[l].astype(BF16),
            "peer_u": peer_u[l].astype(BF16), "peer_v": peer_v[l].astype(BF16),
            "ln3_g": ln3_g[l], "ln3_b": ln3_b[l],
        }
        mem_k = _mm(mem2d, ca_wk[l].astype(BF16), 512, 1024)
        mem_v = _mm(mem2d, ca_wv[l].astype(BF16), 512, 1024)
        yp, sp, _ = _layer(yp, bp, lp, 0, mem_k.reshape(bp, N_MEM, d), mem_v.reshape(bp, N_MEM, d),
                           None, lw, alpha)
        ys, ss, gv = _layer(ys, bs, ls, past_len, cache_mem_k[l], cache_mem_v[l], state_ret[l], lw, alpha)
        mk_l.append(mem_k.reshape(bp, N_MEM, CA_HEADS, CA_DH))
        mv_l.append(mem_v.reshape(bp, N_MEM, CA_HEADS, CA_DH))
        sp_l.append(sp)
        ss_l.append(ss)
        gv_l.append(gv.reshape(bs, ls, A_GROUPS, A_CH))
    return (yp.reshape(bp, lp, d), ys.reshape(bs, ls, d), jnp.stack(mk_l), jnp.stack(mv_l),
            jnp.stack(sp_l), jnp.stack(ss_l), jnp.stack(gv_l))
```

```python
import functools

import jax
import jax.numpy as jnp
from jax import lax
from jax.experimental import pallas as pl
from jax.experimental.pallas import tpu as pltpu

F32 = jnp.float32
BF16 = jnp.bfloat16

A_GROUPS = 4
A_CH = 128
CHUNK = 128
R_HEADS = 4
R_DK = 128
ROPE_BASE = 10000.0
N_MEM = 256
CA_HEADS = 4
CA_DH = 256
PEER_HEADS = 8
PEER_KEYS = 128
PEER_TOPK = 16
LN_EPS = 1e-5
PAST_LEN = 16384
TILE = 128
VMEM_LIMIT = 48 * 1024 * 1024


def _params(*sem):
    return pltpu.CompilerParams(dimension_semantics=sem, vmem_limit_bytes=VMEM_LIMIT)


def _ln(x, g, b):
    mu = jnp.mean(x, axis=-1, keepdims=True)
    xc = x - mu
    var = jnp.mean(xc * xc, axis=-1, keepdims=True)
    return xc * lax.rsqrt(var + LN_EPS) * g + b


def _dot(a, b):
    return jnp.dot(a, b, preferred_element_type=F32)


def _dot_nt(a, b):
    return lax.dot_general(a, b, (((1,), (1,)), ((), ())), preferred_element_type=F32)


def _mm_kernel(x_ref, w_ref, o_ref):
    o_ref[...] = _dot(x_ref[...].astype(BF16), w_ref[...])


def _mm(x, w, tm, tn):
    t, k = x.shape
    n = w.shape[1]
    return pl.pallas_call(
        _mm_kernel,
        grid=(t // tm, n // tn),
        in_specs=[pl.BlockSpec((tm, k), lambda i, j: (i, 0)),
                  pl.BlockSpec((k, tn), lambda i, j: (0, j))],
        out_specs=pl.BlockSpec((tm, tn), lambda i, j: (i, j)),
        out_shape=jax.ShapeDtypeStruct((t, n), F32),
        compiler_params=_params("parallel", "arbitrary"),
        name="mm",
    )(x, w)


def _mm_res_ln_kernel(a_ref, w_ref, x_ref, g_ref, b_ref, o_ref, *, alpha):
    y = _dot(a_ref[...], w_ref[...])
    o_ref[...] = _ln(alpha * x_ref[...] + y, g_ref[...], b_ref[...])


def _mm_res_ln(a, w, x, g, b, alpha, tm):
    t, k = a.shape
    d = w.shape[1]
    return pl.pallas_call(
        functools.partial(_mm_res_ln_kernel, alpha=alpha),
        grid=(t // tm,),
        in_specs=[pl.BlockSpec((tm, k), lambda i: (i, 0)),
                  pl.BlockSpec((k, d), lambda i: (0, 0)),
                  pl.BlockSpec((tm, d), lambda i: (i, 0)),
                  pl.BlockSpec((1, d), lambda i: (0, 0)),
                  pl.BlockSpec((1, d), lambda i: (0, 0))],
        out_specs=pl.BlockSpec((tm, d), lambda i: (i, 0)),
        out_shape=jax.ShapeDtypeStruct((t, d), F32),
        compiler_params=_params("parallel"),
        name="mm_res_ln",
    )(a, w, x, g.reshape(1, d), b.reshape(1, d))


def _mixer_consts(lc, pos):
    half = R_DK // 2
    inv = 1.0 / (ROPE_BASE ** (jnp.arange(half, dtype=F32) / half))
    ang = pos.astype(F32)[:, None] * inv[None, :]
    cos, sin = jnp.cos(ang), jnp.sin(ang)
    rot_c = jnp.concatenate([cos, cos], axis=-1)
    rot_s = jnp.concatenate([-sin, sin], axis=-1)
    lg = jnp.log(1.0 - 2.0 ** (-5.0 - jnp.arange(R_HEADS, dtype=F32)))
    r = jnp.arange(TILE)
    i = (r % lc).astype(F32)
    same = (r[:, None] // lc) == (r[None, :] // lc)
    diff = i[:, None] - i[None, :]
    dm = jnp.where(same[None] & (diff[None] >= 0),
                   jnp.exp(jnp.maximum(diff, 0.0)[None] * lg[:, None, None]), 0.0)
    ones = jnp.ones((1, 1, TILE), F32)
    qd = jnp.exp((i + 1.0)[None, :] * lg[:, None])[:, :, None] * ones
    kd = jnp.exp((lc - 1.0 - i)[None, :] * lg[:, None])[:, :, None] * ones
    cd = jnp.exp(lc * lg)[:, None, None] * jnp.ones((1, 8, TILE), F32)
    return rot_c, rot_s, dm, qd, kd, cd


def _rot(x, c, s):
    return x * c + pltpu.roll(x, R_DK // 2, 1) * s


def _gate_part(h_ref, ws_ref, bsb_ref, glg_ref, glb_ref, mix_ref, van_ref, lc):
    r = lax.broadcasted_iota(jnp.int32, (TILE, TILE), 0)
    c = lax.broadcasted_iota(jnp.int32, (TILE, TILE), 1)
    mask = (c <= r) & ((r // lc) == (c // lc))
    half = A_GROUPS * A_CH
    for g in range(A_GROUPS):
        sl = slice(g * A_CH, (g + 1) * A_CH)
        u = jax.nn.gelu(h_ref[:, sl])
        v = jax.nn.gelu(h_ref[:, half + g * A_CH: half + (g + 1) * A_CH])
        vn = _ln(v, glg_ref[g:g + 1, :], glb_ref[g:g + 1, :])
        w = jnp.where(mask, ws_ref[g], 0.0).astype(BF16)
        mixed = _dot(w, vn.astype(BF16)) + bsb_ref[g]
        mix_ref[:, sl] = (u * mixed).astype(BF16)
        if van_ref is not None:
            van_ref[:, sl] = vn


def _ret_head_inputs(h_ref, hd, rc, rs):
    base = 2 * A_GROUPS * A_CH
    w = R_HEADS * R_DK
    sl = lambda j: slice(base + j * w + hd * R_DK, base + j * w + (hd + 1) * R_DK)
    q = _rot(h_ref[:, sl(0)], rc, rs)
    k = _rot(h_ref[:, sl(1)], rc, rs) * (R_DK ** -0.5)
    return q, k, h_ref[:, sl(2)], h_ref[:, sl(3)]


def _ret_finish(o, gr, gng_ref, gnb_ref, mix_ref, hd):
    on = _ln(o, gng_ref[hd:hd + 1, :], gnb_ref[hd:hd + 1, :])
    off = A_GROUPS * A_CH + hd * R_DK
    mix_ref[:, off:off + R_DK] = (jax.nn.silu(gr) * on).astype(BF16)


def _mixer_prompt_kernel(h_ref, rc_ref, rs_ref, ws_ref, bsb_ref, glg_ref, glb_ref, dm_ref, qd_ref,
                         kd_ref, cd_ref, gng_ref, gnb_ref, mix_ref, s_ref):
    @pl.when(pl.program_id(1) == 0)
    def _():
        s_ref[...] = jnp.zeros_like(s_ref)

    _gate_part(h_ref, ws_ref, bsb_ref, glg_ref, glb_ref, mix_ref, None, CHUNK)
    rc, rs = rc_ref[...], rs_ref[...]
    for hd in range(R_HEADS):
        q, k, v, gr = _ret_head_inputs(h_ref, hd, rc, rs)
        qb, kb, vb = q.astype(BF16), k.astype(BF16), v.astype(BF16)
        inner = _dot_nt(qb, kb) * dm_ref[hd]
        s = s_ref[0, hd]
        o = _dot(inner.astype(BF16), vb) + _dot(qb, s.astype(BF16)) * qd_ref[hd]
        kt = (k * kd_ref[hd]).T.astype(BF16)
        s_ref[0, hd] = s * cd_ref[hd, 0:1, :] + _dot(kt, vb)
        _ret_finish(o, gr, gng_ref, gnb_ref, mix_ref, hd)


def _mixer_sample_kernel(h_ref, rc_ref, rs_ref, ws_ref, bsb_ref, glg_ref, glb_ref, dm_ref, qd_ref,
                         kd_ref, cd_ref, gng_ref, gnb_ref, s0_ref, mix_ref, s_ref, van_ref, *, lc):
    nb = TILE // lc
    _gate_part(h_ref, ws_ref, bsb_ref, glg_ref, glb_ref, mix_ref, van_ref, lc)
    rc, rs = rc_ref[...], rs_ref[...]
    rb = lax.broadcasted_iota(jnp.int32, (TILE, R_DK), 0) // lc
    for hd in range(R_HEADS):
        q, k, v, gr = _ret_head_inputs(h_ref, hd, rc, rs)
        qb, kb, vb = q.astype(BF16), k.astype(BF16), v.astype(BF16)
        inner = _dot_nt(qb, kb) * dm_ref[hd]
        cross = jnp.concatenate(
            [_dot(qb[bi * lc:(bi + 1) * lc], s0_ref[bi, hd].astype(BF16)) for bi in range(nb)], axis=0)
        o = _dot(inner.astype(BF16), vb) + cross * qd_ref[hd]
        kt = (k * kd_ref[hd]).T.astype(BF16)
        cd = cd_ref[hd, 0:1, :]
        for bi in range(nb):
            vm = jnp.where(rb == bi, v, 0.0).astype(BF16)
            s_ref[bi, hd] = s0_ref[bi, hd] * cd + _dot(kt, vm)
        _ret_finish(o, gr, gng_ref, gnb_ref, mix_ref, hd)


def _mixer(h, nseq, seqlen, pos0, s0, w_s, b_s, glg, glb, gng, gnb):
    t, win = h.shape
    dmix = A_GROUPS * A_CH + R_HEADS * R_DK
    if seqlen % CHUNK == 0:
        lc, nchunk = CHUNK, seqlen // CHUNK
        pos = pos0 + jnp.arange(seqlen, dtype=jnp.int32)
    else:
        assert TILE % seqlen == 0 and nseq % (TILE // seqlen) == 0 and s0 is not None
        lc = seqlen
        pos = pos0 + (jnp.arange(TILE, dtype=jnp.int32) % lc)
    rot_c, rot_s, dm, qd, kd, cd = _mixer_consts(lc, pos)
    reps = TILE // lc
    ws_t = jnp.tile(w_s[:, :lc, :lc], (1, reps, reps))
    bsb = jnp.tile(b_s[:, :lc], (1, reps))[:, :, None] * jnp.ones((1, 1, A_CH), F32)
    full = lambda shape: pl.BlockSpec(shape, lambda *_: (0,) * len(shape))
    const_specs = [full((A_GROUPS, TILE, TILE)), full((A_GROUPS, TILE, A_CH)),
                   full((A_GROUPS, A_CH)), full((A_GROUPS, A_CH)),
                   full((R_HEADS, TILE, TILE)), full((R_HEADS, TILE, R_DK)),
                   full((R_HEADS, TILE, R_DK)), full((R_HEADS, 8, R_DK)),
                   full((R_HEADS, R_DK)), full((R_HEADS, R_DK))]
    consts = (ws_t, bsb, glg, glb, dm, qd, kd, cd, gng, gnb)
    if lc == CHUNK:
        mix, s_new = pl.pallas_call(
            _mixer_prompt_kernel,
            grid=(nseq, nchunk),
            in_specs=[pl.BlockSpec((TILE, win), lambda b, n: (b * nchunk + n, 0)),
                      pl.BlockSpec((TILE, R_DK), lambda b, n: (n, 0)),
                      pl.BlockSpec((TILE, R_DK), lambda b, n: (n, 0))] + const_specs,
            out_specs=[pl.BlockSpec((TILE, dmix), lambda b, n: (b * nchunk + n, 0)),
                       pl.BlockSpec((1, R_HEADS, R_DK, R_DK), lambda b, n: (b, 0, 0, 0))],
            out_shape=[jax.ShapeDtypeStruct((t, dmix), BF16),
                       jax.ShapeDtypeStruct((nseq, R_HEADS, R_DK, R_DK), F32)],
            compiler_params=_params("parallel", "arbitrary"),
            name="mixer_prompt",
        )(h, rot_c, rot_s, *consts)
        return mix, s_new, None
    mix, s_new, van = pl.pallas_call(
        functools.partial(_mixer_sample_kernel, lc=lc),
        grid=(t // TILE,),
        in_specs=[pl.BlockSpec((TILE, win), lambda i: (i, 0)),
                  full((TILE, R_DK)), full((TILE, R_DK))] + const_specs
                 + [pl.BlockSpec((reps, R_HEADS, R_DK, R_DK), lambda i: (i, 0, 0, 0))],
        out_specs=[pl.BlockSpec((TILE, dmix), lambda i: (i, 0)),
                   pl.BlockSpec((reps, R_HEADS, R_DK, R_DK), lambda i: (i, 0, 0, 0)),
                   pl.BlockSpec((TILE, A_GROUPS * A_CH), lambda i: (i, 0))],
        out_shape=[jax.ShapeDtypeStruct((t, dmix), BF16),
                   jax.ShapeDtypeStruct((nseq, R_HEADS, R_DK, R_DK), F32),
                   jax.ShapeDtypeStruct((t, A_GROUPS * A_CH), F32)],
        compiler_params=_params("parallel"),
        name="mixer_sample",
    )(h, rot_c, rot_s, *consts, s0)
    return mix, s_new, van


def _attn_kernel(q_ref, k_ref, v_ref, o_ref, *, nb, rows):
    split = len(k_ref.shape) == 4
    for bi in range(nb):
        rs = slice(bi * rows, (bi + 1) * rows)
        for hh in range(CA_HEADS):
            sl = slice(hh * CA_DH, (hh + 1) * CA_DH)
            kh = k_ref[bi, :, hh, :] if split else k_ref[bi, :, sl]
            vh = v_ref[bi, :, hh, :] if split else v_ref[bi, :, sl]
            sc = _dot_nt(q_ref[rs, sl].astype(BF16), kh.astype(BF16)) * (CA_DH ** -0.5)
            e = jnp.exp(sc - jnp.max(sc, axis=-1, keepdims=True))
            p = e / jnp.sum(e, axis=-1, keepdims=True)
            o_ref[rs, sl] = _dot(p.astype(BF16), vh.astype(BF16)).astype(BF16)


def _attn(qc, mem_k, mem_v, nseq, seqlen):
    t, d = qc.shape
    kv_tail = mem_k.shape[1:]
    zeros = (0,) * len(kv_tail)
    if seqlen >= 512:
        nb, rows = 1, 512
        per_seq = seqlen // rows
        grid = (nseq, per_seq)
        q_map = lambda b, i: (b * per_seq + i, 0)
        kv_map = lambda b, i: (b,) + zeros
        sem = ("parallel", "arbitrary")
    else:
        nb, rows = 8, seqlen
        grid = (nseq // nb,)
        q_map = lambda i: (i, 0)
        kv_map = lambda i: (i,) + zeros
        sem = ("parallel",)
    return pl.pallas_call(
        functools.partial(_attn_kernel, nb=nb, rows=rows),
        grid=grid,
        in_specs=[pl.BlockSpec((nb * rows, d), q_map),
                  pl.BlockSpec((nb,) + kv_tail, kv_map),
                  pl.BlockSpec((nb,) + kv_tail, kv_map)],
        out_specs=pl.BlockSpec((nb * rows, d), q_map),
        out_shape=jax.ShapeDtypeStruct((t, d), BF16),
        compiler_params=_params(*sem),
        name="attn",
    )(qc, mem_k, mem_v)


_ROW_LEN = [PEER_TOPK // (k1 + 1) for k1 in range(PEER_TOPK)]
_NCAND = sum(_ROW_LEN)
_NCAND_PAD = -(-_NCAND // 8) * 8


def _topk_rows(s, k):
    n = s.shape[0]
    iota = lax.broadcasted_iota(jnp.int32, s.shape, 0).astype(F32)
    vals, idxs = [], []
    for _ in range(k):
        m = jnp.max(s, axis=0, keepdims=True)
        idx = jnp.min(jnp.where(s == m, iota, float(n)), axis=0, keepdims=True)
        vals.append(m)
        idxs.append(idx)
        s = jnp.where(iota == idx, -jnp.inf, s)
    return vals, idxs


def _route_select(sv1, si1, sv2, si2):
    w = sv1[0].shape[1]
    v2 = jnp.concatenate(sv2, axis=0)
    i2 = jnp.concatenate(si2, axis=0)
    cv, ca, cb = [], [], []
    for k1, n in enumerate(_ROW_LEN):
        cv.append(sv1[k1] + v2[0:n])
        ca.append(jnp.broadcast_to(si1[k1], (n, w)))
        cb.append(i2[0:n])
    pad = _NCAND_PAD - _NCAND
    if pad:
        cv.append(jnp.full((pad, w), -jnp.inf, F32))
        ca.append(jnp.zeros((pad, w), F32))
        cb.append(jnp.zeros((pad, w), F32))
    cand = jnp.concatenate(cv, axis=0)
    ca = jnp.concatenate(ca, axis=0)
    cb = jnp.concatenate(cb, axis=0)
    iota = lax.broadcasted_iota(jnp.int32, cand.shape, 0).astype(F32)
    fv, fa, fb = [], [], []
    for _ in range(PEER_TOPK):
        m = jnp.max(cand, axis=0, keepdims=True)
        idx = jnp.min(jnp.where(cand == m, iota, float(_NCAND_PAD)), axis=0, keepdims=True)
        sel = iota == idx
        fv.append(m)
        fa.append(jnp.sum(jnp.where(sel, ca, 0.0), axis=0, keepdims=True))
        fb.append(jnp.sum(jnp.where(sel, cb, 0.0), axis=0, keepdims=True))
        cand = jnp.where(sel, -jnp.inf, cand)
    fv = jnp.concatenate(fv, axis=0)
    e = jnp.exp(fv - fv[0:1])
    g = e / jnp.sum(e, axis=0, keepdims=True)
    to_int = lambda rows: jnp.concatenate(rows, axis=0).astype(jnp.int32)
    return to_int(fa), to_int(fb), g


_ID_OFF = 128.0
_TIE = 2 * _ID_OFF


def _col_max(x):
    tiles = [x[8 * i:8 * i + 8] for i in range(x.shape[0] // 8)]
    while len(tiles) > 1:
        tiles = [jnp.maximum(tiles[i], tiles[i + 1]) for i in range(0, len(tiles) - 1, 2)] + tiles[len(tiles) & ~1:]
    return jnp.max(tiles[0], axis=0, keepdims=True)


def _topk_unique(scores, ids_lhs, sub):
    zero = jnp.zeros((8, scores[0].shape[1]), F32)
    scores = list(scores)
    out = [([zero, zero], [zero, zero], zero) for _ in scores]
    for k in range(PEER_TOPK):
        here = sub == (k % 8)
        for j, s in enumerate(scores):
            v, r, worst = out[j]
            m = _col_max(s)
            e = s == m
            scores[j] = jnp.where(e, -jnp.inf, s)
            rk = _dot(ids_lhs, jnp.where(e, 1.0, 0.0).astype(BF16))[0:8]
            v[k // 8] = jnp.where(here, m, v[k // 8])
            r[k // 8] = jnp.where(here, rk, r[k // 8])
            out[j] = (v, r, jnp.maximum(worst, rk))
    return out


def _select_unique(v1, r1, v2, r2, ones_lhs, sub):
    w = v1[0].shape[1]
    bc = lambda x, row: jnp.broadcast_to(x[row:row + 1], (8, w))
    a1 = [r1[0] - _ID_OFF, r1[1] - _ID_OFF]
    v10, a10 = bc(v1[0], 0), bc(a1[0], 0)
    cand, ca, cb = [v10 + v2[0], v10 + v2[1]], [a10, a10], [r2[0], r2[1]]
    for k1 in range(1, 8):
        c = bc(v1[0], k1) + v2[0]
        if _ROW_LEN[k1] < 8:
            c = jnp.where(sub < _ROW_LEN[k1], c, -jnp.inf)
        cand.append(c)
        ca.append(bc(a1[0], k1))
        cb.append(r2[0])
    cand.append(v1[1] + bc(v2[0], 0))
    ca.append(a1[1])
    cb.append(bc(r2[0], 0))
    cand = jnp.concatenate(cand, axis=0)
    ca = jnp.concatenate(ca, axis=0)
    cb = jnp.concatenate(cb, axis=0)
    zero = jnp.zeros((8, w), F32)
    fv, fa, fb, worst = [zero, zero], [zero, zero], [zero, zero], zero
    for k in range(PEER_TOPK):
        m = _col_max(cand)
        e = cand == m
        cand = jnp.where(e, -jnp.inf, cand)
        ra = _dot(ones_lhs, jnp.where(e, ca, 0.0).astype(BF16))[0:8]
        rb = _dot(ones_lhs, jnp.where(e, cb, 0.0).astype(BF16))[0:8]
        worst = jnp.maximum(worst, rb)
        here = sub == (k % 8)
        fv[k // 8] = jnp.where(here, m, fv[k // 8])
        fa[k // 8] = jnp.where(here, ra, fa[k // 8])
        fb[k // 8] = jnp.where(here, rb, fb[k // 8])
    fv = jnp.concatenate(fv, axis=0)
    ex = jnp.exp(fv - fv[0:1])
    g = ex / jnp.sum(ex, axis=0, keepdims=True)
    a = jnp.concatenate(fa, axis=0).astype(jnp.int32)
    b = (jnp.concatenate(fb, axis=0) - _ID_OFF).astype(jnp.int32)
    return a, b, g, worst


def _route_kernel(x_ref, wq_ref, sk_ref, a_ref, b_ref, g_ref, s_ref, at_ref, bt_ref, gt_ref, *, tm):
    nch = tm // TILE
    sub = lax.broadcasted_iota(jnp.int32, (8, TILE), 0)
    ids_lhs = (lax.broadcasted_iota(jnp.int32, (16, PEER_KEYS), 1).astype(F32) + _ID_OFF).astype(BF16)
    ones_lhs = jnp.ones((16, 10 * 8), BF16)

    q = _dot(x_ref[...].astype(BF16), wq_ref[...]).astype(BF16)
    for hh in range(PEER_HEADS):
        for ch in range(nch):
            for c in range(2):
                col = (2 * hh + c) * PEER_KEYS
                s_ref[hh, 2 * ch + c] = _dot_nt(sk_ref[hh, c], q[ch * TILE:(ch + 1) * TILE, col:col + PEER_KEYS])

    def store(hh, ch, a, b, g):
        row = pl.multiple_of(hh * PEER_TOPK, PEER_TOPK)
        cols = slice(ch * TILE, (ch + 1) * TILE)
        at_ref[pl.ds(row, PEER_TOPK), cols] = a
        bt_ref[pl.ds(row, PEER_TOPK), cols] = b
        gt_ref[pl.ds(row, PEER_TOPK), cols] = g

    def head_unique(hh, worst):
        for ch in range(nch):
            (v1, r1, w1), (v2, r2, w2) = _topk_unique([s_ref[hh, 2 * ch], s_ref[hh, 2 * ch + 1]], ids_lhs, sub)
            a, b, g, w3 = _select_unique(v1, r1, v2, r2, ones_lhs, sub)
            store(hh, ch, a, b, g)
            worst = jnp.maximum(jnp.maximum(worst, w1), jnp.maximum(w2, w3))
        return worst

    worst = lax.fori_loop(0, PEER_HEADS, head_unique, jnp.zeros((8, TILE), F32))

    @pl.when(jnp.max(worst) >= _TIE)
    def _():
        def head_exact(hh, carry):
            for ch in range(nch):
                tops = [_topk_rows(s_ref[hh, 2 * ch + c], PEER_TOPK) for c in range(2)]
                store(hh, ch, *_route_select(tops[0][0], tops[0][1], tops[1][0], tops[1][1]))
            return carry

        lax.fori_loop(0, PEER_HEADS, head_exact, 0)

    for ch in range(nch):
        rows = slice(ch * TILE, (ch + 1) * TILE)
        a_ref[rows, :] = at_ref[:, rows].T
        b_ref[rows, :] = bt_ref[:, rows].T
        g_ref[rows, :] = gt_ref[:, rows].T


def _route(x, wq, sk, tm):
    t, d = x.shape
    nsel = PEER_HEADS * PEER_TOPK
    spec = pl.BlockSpec((tm, nsel), lambda i: (i, 0))
    return pl.pallas_call(
        functools.partial(_route_kernel, tm=tm),
        grid=(t // tm,),
        in_specs=[pl.BlockSpec((tm, d), lambda i: (i, 0)),
                  pl.BlockSpec(wq.shape, lambda i: (0, 0)),
                  pl.BlockSpec(sk.shape, lambda i: (0, 0, 0, 0))],
        out_specs=[spec, spec, spec],
        out_shape=[jax.ShapeDtypeStruct((t, nsel), jnp.int32),
                   jax.ShapeDtypeStruct((t, nsel), jnp.int32),
                   jax.ShapeDtypeStruct((t, nsel), F32)],
        scratch_shapes=[pltpu.VMEM((PEER_HEADS, 2 * (tm // TILE), PEER_KEYS, TILE), F32),
                        pltpu.VMEM((nsel, tm), jnp.int32), pltpu.VMEM((nsel, tm), jnp.int32),
                        pltpu.VMEM((nsel, tm), F32)],
        compiler_params=_params("parallel"),
        name="peer_route",
    )(x, wq, sk)


_WG_PITCH = PEER_KEYS + 8
_WG_GROUP = 16


def _wg_kernel(a_ref, b_ref, g_ref, o_ref, w_ref, *, tb):
    iota = lax.broadcasted_iota(jnp.int32, (PEER_KEYS, PEER_HEADS * PEER_TOPK), 0)
    for g0 in range(0, tb, _WG_GROUP):
        for t in range(g0, g0 + _WG_GROUP):
            pa = jnp.where(iota == a_ref[t:t + 1, :], 1.0, 0.0).astype(BF16)
            pb = jnp.where(iota == b_ref[t:t + 1, :], g_ref[t:t + 1, :], 0.0).astype(BF16)
            w_ref[t * _WG_PITCH:t * _WG_PITCH + PEER_KEYS, :] = _dot_nt(pa, pb)
        for a in range(PEER_KEYS):
            rows = w_ref[pl.ds(g0 * _WG_PITCH + a, _WG_GROUP, stride=_WG_PITCH), :]
            o_ref[g0:g0 + _WG_GROUP, a * PEER_KEYS:(a + 1) * PEER_KEYS] = rows.astype(BF16)


def _wg(a, b, g, tb):
    t, nsel = a.shape
    spec = pl.BlockSpec((tb, nsel), lambda i: (i, 0))
    return pl.pallas_call(
        functools.partial(_wg_kernel, tb=tb),
        grid=(t // tb,),
        in_specs=[spec, spec, spec],
        out_specs=pl.BlockSpec((tb, PEER_KEYS * PEER_KEYS), lambda i: (i, 0)),
        out_shape=jax.ShapeDtypeStruct((t, PEER_KEYS * PEER_KEYS), BF16),
        scratch_shapes=[pltpu.VMEM((tb * _WG_PITCH, PEER_KEYS), F32)],
        compiler_params=_params("parallel"),
        name="peer_wg",
    )(a, b, g)


def _experts_kernel(x_ref, wg_ref, u_ref, v_ref, g_ref, b_ref, o_ref, xb_ref, acc_ref, *, alpha):
    e = pl.program_id(1)

    @pl.when(e == 0)
    def _():
        xb_ref[...] = x_ref[...].astype(BF16)
        acc_ref[...] = jnp.zeros_like(acc_ref)

    act = jax.nn.gelu(_dot_nt(xb_ref[...], u_ref[...]))
    p = (act * wg_ref[...].astype(F32)).astype(BF16)
    acc_ref[...] += _dot(p, v_ref[...])

    @pl.when(e == pl.num_programs(1) - 1)
    def _():
        o_ref[...] = _ln(alpha * x_ref[...] + acc_ref[...], g_ref[...], b_ref[...])


def _experts(x, wg, u, v, g, b, alpha, tm, eb):
    t, d = x.shape
    ne = u.shape[0]
    return pl.pallas_call(
        functools.partial(_experts_kernel, alpha=alpha),
        grid=(t // tm, ne // eb),
        in_specs=[pl.BlockSpec((tm, d), lambda i, e: (i, 0)),
                  pl.BlockSpec((tm, eb), lambda i, e: (i, e)),
                  pl.BlockSpec((eb, d), lambda i, e: (e, 0)),
                  pl.BlockSpec((eb, d), lambda i, e: (e, 0)),
                  pl.BlockSpec((1, d), lambda i, e: (0, 0)),
                  pl.BlockSpec((1, d), lambda i, e: (0, 0))],
        out_specs=pl.BlockSpec((tm, d), lambda i, e: (i, 0)),
        out_shape=jax.ShapeDtypeStruct((t, d), F32),
        scratch_shapes=[pltpu.VMEM((tm, d), BF16), pltpu.VMEM((tm, d), F32)],
        compiler_params=_params("parallel", "arbitrary"),
        name="peer_experts",
    )(x, wg, u, v, g.reshape(1, d), b.reshape(1, d))


def _layer(x, nseq, seqlen, pos0, mem_k, mem_v, s0, lw, alpha):
    t, d = x.shape
    tm = min(512, t)
    h = _mm(x, lw["w_in"], tm, 1024)
    mix, s_new, van = _mixer(h, nseq, seqlen, pos0, s0, lw["w_s"], lw["b_s"], lw["gate_ln_g"],
                             lw["gate_ln_b"], lw["ret_gn_g"], lw["ret_gn_b"])
    x = _mm_res_ln(mix, lw["w_o"], x, lw["ln1_g"], lw["ln1_b"], alpha, tm)
    qc = _mm(x, lw["ca_wq"], tm, 1024)
    ca = _attn(qc, mem_k, mem_v, nseq, seqlen)
    x = _mm_res_ln(ca, lw["ca_wo"], x, lw["ln2_g"], lw["ln2_b"], alpha, tm)
    a, b, g = _route(x, lw["peer_wq"], lw["peer_subkeys"], min(512, t))
    wg = _wg(a, b, g, 64)
    x = _experts(x, wg, lw["peer_u"], lw["peer_v"], lw["ln3_g"], lw["ln3_b"], alpha, tm, 1024)
    return x, s_new, van


def kernel(x_prompt, x_sample, mem_prompt, cache_mem_k, cache_mem_v, state_ret, w_in, w_s, b_s, gate_ln_g, gate_ln_b, ret_gn_g, ret_gn_b, w_o, ln1_g, ln1_b, ca_wq, ca_wk, ca_wv, ca_wo, ln2_g, ln2_b, peer_wq, peer_subkeys, peer_u, peer_v, ln3_g, ln3_b):
    depth = w_in.shape[0]
    bp, lp, d = x_prompt.shape
    bs, ls, _ = x_sample.shape
    alpha = (2 * depth) ** 0.25
    yp = x_prompt.reshape(bp * lp, d)
    ys = x_sample.reshape(bs * ls, d)
    mem2d = mem_prompt.reshape(bp * N_MEM, d)
    past_len = PAST_LEN
    mk_l, mv_l, sp_l, ss_l, gv_l = [], [], [], [], []
    for l in range(depth):
        lw = {
            "w_in": w_in[l].astype(BF16), "w_s": w_s[l], "b_s": b_s[l],
            "gate_ln_g": gate_ln_g[l], "gate_ln_b": gate_ln_b[l],
            "ret_gn_g": ret_gn_g[l], "ret_gn_b": ret_gn_b[l],
            "w_o": w_o[l].astype(BF16), "ln1_g": ln1_g[l], "ln1_b": ln1_b[l],
            "ca_wq": ca_wq[l].astype(BF16), "ca_wo": ca_wo[l].astype(BF16),
            "ln2_g": ln2_g[l], "ln2_b": ln2_b[l],
            "peer_wq": peer_wq[l].astype(BF16),
            "peer_subkeys": peer_subkeys[l].astype(BF16),
            "peer_u": peer_u[l].astype(BF16), "peer_v": peer_v[l].astype(BF16),
            "ln3_g": ln3_g[l], "ln3_b": ln3_b[l],
        }
        mem_k = _mm(mem2d, ca_wk[l].astype(BF16), 512, 1024)
        mem_v = _mm(mem2d, ca_wv[l].astype(BF16), 512, 1024)
        yp, sp, _ = _layer(yp, bp, lp, 0, mem_k.reshape(bp, N_MEM, d), mem_v.reshape(bp, N_MEM, d),
                           None, lw, alpha)
        ys, ss, gv = _layer(ys, bs, ls, past_len, cache_mem_k[l].reshape(bs, N_MEM, d).astype(BF16),
                            cache_mem_v[l].reshape(bs, N_MEM, d).astype(BF16), state_ret[l], lw, alpha)
        mk_l.append(mem_k.reshape(bp, N_MEM, CA_HEADS, CA_DH))
        mv_l.append(mem_v.reshape(bp, N_MEM, CA_HEADS, CA_DH))
        sp_l.append(sp)
        ss_l.append(ss)
        gv_l.append(gv.reshape(bs, ls, A_GROUPS, A_CH))
    return (yp.reshape(bp, lp, d), ys.reshape(bs, ls, d), jnp.stack(mk_l), jnp.stack(mv_l),
            jnp.stack(sp_l), jnp.stack(ss_l), jnp.stack(gv_l))
```

```python
import functools

import jax
import jax.numpy as jnp
from jax import lax
from jax.experimental import pallas as pl
from jax.experimental.pallas import tpu as pltpu

F32 = jnp.float32
BF16 = jnp.bfloat16

A_GROUPS = 4
A_CH = 128
CHUNK = 128
R_HEADS = 4
R_DK = 128
ROPE_BASE = 10000.0
N_MEM = 256
CA_HEADS = 4
CA_DH = 256
PEER_HEADS = 8
PEER_KEYS = 128
PEER_TOPK = 16
LN_EPS = 1e-5
PAST_LEN = 16384
TILE = 128
VMEM_LIMIT = 48 * 1024 * 1024


def _params(*sem):
    return pltpu.CompilerParams(dimension_semantics=sem, vmem_limit_bytes=VMEM_LIMIT)


def _ln(x, g, b):
    mu = jnp.mean(x, axis=-1, keepdims=True)
    xc = x - mu
    var = jnp.mean(xc * xc, axis=-1, keepdims=True)
    return xc * lax.rsqrt(var + LN_EPS) * g + b


def _dot(a, b):
    return jnp.dot(a, b, preferred_element_type=F32)


def _dot_nt(a, b):
    return lax.dot_general(a, b, (((1,), (1,)), ((), ())), preferred_element_type=F32)


def _mm_kernel(x_ref, w_ref, o_ref):
    o_ref[...] = _dot(x_ref[...].astype(BF16), w_ref[...])


def _mm(x, w, tm, tn):
    t, k = x.shape
    n = w.shape[1]
    return pl.pallas_call(
        _mm_kernel,
        grid=(t // tm, n // tn),
        in_specs=[pl.BlockSpec((tm, k), lambda i, j: (i, 0)),
                  pl.BlockSpec((k, tn), lambda i, j: (0, j))],
        out_specs=pl.BlockSpec((tm, tn), lambda i, j: (i, j)),
        out_shape=jax.ShapeDtypeStruct((t, n), F32),
        compiler_params=_params("parallel", "arbitrary"),
        name="mm",
    )(x, w)


def _mm_res_ln_kernel(a_ref, w_ref, x_ref, g_ref, b_ref, o_ref, *, alpha):
    y = _dot(a_ref[...], w_ref[...])
    o_ref[...] = _ln(alpha * x_ref[...] + y, g_ref[...], b_ref[...])


def _mm_res_ln(a, w, x, g, b, alpha, tm):
    t, k = a.shape
    d = w.shape[1]
    return pl.pallas_call(
        functools.partial(_mm_res_ln_kernel, alpha=alpha),
        grid=(t // tm,),
        in_specs=[pl.BlockSpec((tm, k), lambda i: (i, 0)),
                  pl.BlockSpec((k, d), lambda i: (0, 0)),
                  pl.BlockSpec((tm, d), lambda i: (i, 0)),
                  pl.BlockSpec((1, d), lambda i: (0, 0)),
                  pl.BlockSpec((1, d), lambda i: (0, 0))],
        out_specs=pl.BlockSpec((tm, d), lambda i: (i, 0)),
        out_shape=jax.ShapeDtypeStruct((t, d), F32),
        compiler_params=_params("parallel"),
        name="mm_res_ln",
    )(a, w, x, g.reshape(1, d), b.reshape(1, d))


def _mixer_consts(lc, pos):
    half = R_DK // 2
    inv = 1.0 / (ROPE_BASE ** (jnp.arange(half, dtype=F32) / half))
    ang = pos.astype(F32)[:, None] * inv[None, :]
    cos, sin = jnp.cos(ang), jnp.sin(ang)
    rot_c = jnp.concatenate([cos, cos], axis=-1)
    rot_s = jnp.concatenate([-sin, sin], axis=-1)
    lg = jnp.log(1.0 - 2.0 ** (-5.0 - jnp.arange(R_HEADS, dtype=F32)))
    r = jnp.arange(TILE)
    i = (r % lc).astype(F32)
    same = (r[:, None] // lc) == (r[None, :] // lc)
    diff = i[:, None] - i[None, :]
    dm = jnp.where(same[None] & (diff[None] >= 0),
                   jnp.exp(jnp.maximum(diff, 0.0)[None] * lg[:, None, None]), 0.0)
    ones = jnp.ones((1, 1, TILE), F32)
    qd = jnp.exp((i + 1.0)[None, :] * lg[:, None])[:, :, None] * ones
    kd = jnp.exp((lc - 1.0 - i)[None, :] * lg[:, None])[:, :, None] * ones
    cd = jnp.exp(lc * lg)[:, None, None] * jnp.ones((1, 8, TILE), F32)
    return rot_c, rot_s, dm, qd, kd, cd


def _rot(x, c, s):
    return x * c + pltpu.roll(x, R_DK // 2, 1) * s


def _gate_part(h_ref, ws_ref, bsb_ref, glg_ref, glb_ref, mix_ref, van_ref, lc):
    r = lax.broadcasted_iota(jnp.int32, (TILE, TILE), 0)
    c = lax.broadcasted_iota(jnp.int32, (TILE, TILE), 1)
    mask = (c <= r) & ((r // lc) == (c // lc))
    half = A_GROUPS * A_CH
    for g in range(A_GROUPS):
        sl = slice(g * A_CH, (g + 1) * A_CH)
        u = jax.nn.gelu(h_ref[:, sl])
        v = jax.nn.gelu(h_ref[:, half + g * A_CH: half + (g + 1) * A_CH])
        vn = _ln(v, glg_ref[g:g + 1, :], glb_ref[g:g + 1, :])
        w = jnp.where(mask, ws_ref[g], 0.0).astype(BF16)
        mixed = _dot(w, vn.astype(BF16)) + bsb_ref[g]
        mix_ref[:, sl] = (u * mixed).astype(BF16)
        if van_ref is not None:
            van_ref[:, sl] = vn


def _ret_head_inputs(h_ref, hd, rc, rs):
    base = 2 * A_GROUPS * A_CH
    w = R_HEADS * R_DK
    sl = lambda j: slice(base + j * w + hd * R_DK, base + j * w + (hd + 1) * R_DK)
    q = _rot(h_ref[:, sl(0)], rc, rs)
    k = _rot(h_ref[:, sl(1)], rc, rs) * (R_DK ** -0.5)
    return q, k, h_ref[:, sl(2)], h_ref[:, sl(3)]


def _ret_finish(o, gr, gng_ref, gnb_ref, mix_ref, hd):
    on = _ln(o, gng_ref[hd:hd + 1, :], gnb_ref[hd:hd + 1, :])
    off = A_GROUPS * A_CH + hd * R_DK
    mix_ref[:, off:off + R_DK] = (jax.nn.silu(gr) * on).astype(BF16)


def _mixer_prompt_kernel(h_ref, rc_ref, rs_ref, ws_ref, bsb_ref, glg_ref, glb_ref, dm_ref, qd_ref,
                         kd_ref, cd_ref, gng_ref, gnb_ref, mix_ref, s_ref):
    @pl.when(pl.program_id(1) == 0)
    def _():
        s_ref[...] = jnp.zeros_like(s_ref)

    _gate_part(h_ref, ws_ref, bsb_ref, glg_ref, glb_ref, mix_ref, None, CHUNK)
    rc, rs = rc_ref[...], rs_ref[...]
    for hd in range(R_HEADS):
        q, k, v, gr = _ret_head_inputs(h_ref, hd, rc, rs)
        qb, kb, vb = q.astype(BF16), k.astype(BF16), v.astype(BF16)
        inner = _dot_nt(qb, kb) * dm_ref[hd]
        s = s_ref[0, hd]
        o = _dot(inner.astype(BF16), vb) + _dot(qb, s.astype(BF16)) * qd_ref[hd]
        kt = (k * kd_ref[hd]).T.astype(BF16)
        s_ref[0, hd] = s * cd_ref[hd, 0:1, :] + _dot(kt, vb)
        _ret_finish(o, gr, gng_ref, gnb_ref, mix_ref, hd)


def _mixer_sample_kernel(h_ref, rc_ref, rs_ref, ws_ref, bsb_ref, glg_ref, glb_ref, dm_ref, qd_ref,
                         kd_ref, cd_ref, gng_ref, gnb_ref, s0_ref, mix_ref, s_ref, van_ref, *, lc):
    nb = TILE // lc
    _gate_part(h_ref, ws_ref, bsb_ref, glg_ref, glb_ref, mix_ref, van_ref, lc)
    rc, rs = rc_ref[...], rs_ref[...]
    rb = lax.broadcasted_iota(jnp.int32, (TILE, R_DK), 0) // lc
    for hd in range(R_HEADS):
        q, k, v, gr = _ret_head_inputs(h_ref, hd, rc, rs)
        qb, kb, vb = q.astype(BF16), k.astype(BF16), v.astype(BF16)
        inner = _dot_nt(qb, kb) * dm_ref[hd]
        cross = jnp.concatenate(
            [_dot(qb[bi * lc:(bi + 1) * lc], s0_ref[bi, hd].astype(BF16)) for bi in range(nb)], axis=0)
        o = _dot(inner.astype(BF16), vb) + cross * qd_ref[hd]
        kt = (k * kd_ref[hd]).T.astype(BF16)
        cd = cd_ref[hd, 0:1, :]
        for bi in range(nb):
            vm = jnp.where(rb == bi, v, 0.0).astype(BF16)
            s_ref[bi, hd] = s0_ref[bi, hd] * cd + _dot(kt, vm)
        _ret_finish(o, gr, gng_ref, gnb_ref, mix_ref, hd)


def _mixer(h, nseq, seqlen, pos0, s0, w_s, b_s, glg, glb, gng, gnb):
    t, win = h.shape
    dmix = A_GROUPS * A_CH + R_HEADS * R_DK
    if seqlen % CHUNK == 0:
        lc, nchunk = CHUNK, seqlen // CHUNK
        pos = pos0 + jnp.arange(seqlen, dtype=jnp.int32)
    else:
        assert TILE % seqlen == 0 and nseq % (TILE // seqlen) == 0 and s0 is not None
        lc = seqlen
        pos = pos0 + (jnp.arange(TILE, dtype=jnp.int32) % lc)
    rot_c, rot_s, dm, qd, kd, cd = _mixer_consts(lc, pos)
    reps = TILE // lc
    ws_t = jnp.tile(w_s[:, :lc, :lc], (1, reps, reps))
    bsb = jnp.tile(b_s[:, :lc], (1, reps))[:, :, None] * jnp.ones((1, 1, A_CH), F32)
    full = lambda shape: pl.BlockSpec(shape, lambda *_: (0,) * len(shape))
    const_specs = [full((A_GROUPS, TILE, TILE)), full((A_GROUPS, TILE, A_CH)),
                   full((A_GROUPS, A_CH)), full((A_GROUPS, A_CH)),
                   full((R_HEADS, TILE, TILE)), full((R_HEADS, TILE, R_DK)),
                   full((R_HEADS, TILE, R_DK)), full((R_HEADS, 8, R_DK)),
                   full((R_HEADS, R_DK)), full((R_HEADS, R_DK))]
    consts = (ws_t, bsb, glg, glb, dm, qd, kd, cd, gng, gnb)
    if lc == CHUNK:
        mix, s_new = pl.pallas_call(
            _mixer_prompt_kernel,
            grid=(nseq, nchunk),
            in_specs=[pl.BlockSpec((TILE, win), lambda b, n: (b * nchunk + n, 0)),
                      pl.BlockSpec((TILE, R_DK), lambda b, n: (n, 0)),
                      pl.BlockSpec((TILE, R_DK), lambda b, n: (n, 0))] + const_specs,
            out_specs=[pl.BlockSpec((TILE, dmix), lambda b, n: (b * nchunk + n, 0)),
                       pl.BlockSpec((1, R_HEADS, R_DK, R_DK), lambda b, n: (b, 0, 0, 0))],
            out_shape=[jax.ShapeDtypeStruct((t, dmix), BF16),
                       jax.ShapeDtypeStruct((nseq, R_HEADS, R_DK, R_DK), F32)],
            compiler_params=_params("parallel", "arbitrary"),
            name="mixer_prompt",
        )(h, rot_c, rot_s, *consts)
        return mix, s_new, None
    mix, s_new, van = pl.pallas_call(
        functools.partial(_mixer_sample_kernel, lc=lc),
        grid=(t // TILE,),
        in_specs=[pl.BlockSpec((TILE, win), lambda i: (i, 0)),
                  full((TILE, R_DK)), full((TILE, R_DK))] + const_specs
                 + [pl.BlockSpec((reps, R_HEADS, R_DK, R_DK), lambda i: (i, 0, 0, 0))],
        out_specs=[pl.BlockSpec((TILE, dmix), lambda i: (i, 0)),
                   pl.BlockSpec((reps, R_HEADS, R_DK, R_DK), lambda i: (i, 0, 0, 0)),
                   pl.BlockSpec((TILE, A_GROUPS * A_CH), lambda i: (i, 0))],
        out_shape=[jax.ShapeDtypeStruct((t, dmix), BF16),
                   jax.ShapeDtypeStruct((nseq, R_HEADS, R_DK, R_DK), F32),
                   jax.ShapeDtypeStruct((t, A_GROUPS * A_CH), F32)],
        compiler_params=_params("parallel"),
        name="mixer_sample",
    )(h, rot_c, rot_s, *consts, s0)
    return mix, s_new, van


def _attn_kernel(q_ref, k_ref, v_ref, o_ref, *, nb, rows):
    split = len(k_ref.shape) == 4
    for bi in range(nb):
        rs = slice(bi * rows, (bi + 1) * rows)
        for hh in range(CA_HEADS):
            sl = slice(hh * CA_DH, (hh + 1) * CA_DH)
            kh = k_ref[bi, :, hh, :] if split else k_ref[bi, :, sl]
            vh = v_ref[bi, :, hh, :] if split else v_ref[bi, :, sl]
            sc = _dot_nt(q_ref[rs, sl].astype(BF16), kh.astype(BF16)) * (CA_DH ** -0.5)
            e = jnp.exp(sc - jnp.max(sc, axis=-1, keepdims=True))
            p = e / jnp.sum(e, axis=-1, keepdims=True)
            o_ref[rs, sl] = _dot(p.astype(BF16), vh.astype(BF16)).astype(BF16)


def _attn(qc, mem_k, mem_v, nseq, seqlen):
    t, d = qc.shape
    kv_tail = mem_k.shape[1:]
    zeros = (0,) * len(kv_tail)
    if seqlen >= 512:
        nb, rows = 1, 512
        per_seq = seqlen // rows
        grid = (nseq, per_seq)
        q_map = lambda b, i: (b * per_seq + i, 0)
        kv_map = lambda b, i: (b,) + zeros
        sem = ("parallel", "arbitrary")
    else:
        nb, rows = 4, seqlen
        grid = (nseq // nb,)
        q_map = lambda i: (i, 0)
        kv_map = lambda i: (i,) + zeros
        sem = ("parallel",)
    return pl.pallas_call(
        functools.partial(_attn_kernel, nb=nb, rows=rows),
        grid=grid,
        in_specs=[pl.BlockSpec((nb * rows, d), q_map),
                  pl.BlockSpec((nb,) + kv_tail, kv_map),
                  pl.BlockSpec((nb,) + kv_tail, kv_map)],
        out_specs=pl.BlockSpec((nb * rows, d), q_map),
        out_shape=jax.ShapeDtypeStruct((t, d), BF16),
        compiler_params=_params(*sem),
        name="attn",
    )(qc, mem_k, mem_v)


_ROW_LEN = [PEER_TOPK // (k1 + 1) for k1 in range(PEER_TOPK)]
_NCAND = sum(_ROW_LEN)
_NCAND_PAD = -(-_NCAND // 8) * 8


def _topk_rows(s, k):
    n = s.shape[0]
    iota = lax.broadcasted_iota(jnp.int32, s.shape, 0).astype(F32)
    vals, idxs = [], []
    for _ in range(k):
        m = jnp.max(s, axis=0, keepdims=True)
        idx = jnp.min(jnp.where(s == m, iota, float(n)), axis=0, keepdims=True)
        vals.append(m)
        idxs.append(idx)
        s = jnp.where(iota == idx, -jnp.inf, s)
    return vals, idxs


def _route_select(sv1, si1, sv2, si2):
    w = sv1[0].shape[1]
    v2 = jnp.concatenate(sv2, axis=0)
    i2 = jnp.concatenate(si2, axis=0)
    cv, ca, cb = [], [], []
    for k1, n in enumerate(_ROW_LEN):
        cv.append(sv1[k1] + v2[0:n])
        ca.append(jnp.broadcast_to(si1[k1], (n, w)))
        cb.append(i2[0:n])
    pad = _NCAND_PAD - _NCAND
    if pad:
        cv.append(jnp.full((pad, w), -jnp.inf, F32))
        ca.append(jnp.zeros((pad, w), F32))
        cb.append(jnp.zeros((pad, w), F32))
    cand = jnp.concatenate(cv, axis=0)
    ca = jnp.concatenate(ca, axis=0)
    cb = jnp.concatenate(cb, axis=0)
    iota = lax.broadcasted_iota(jnp.int32, cand.shape, 0).astype(F32)
    fv, fa, fb = [], [], []
    for _ in range(PEER_TOPK):
        m = jnp.max(cand, axis=0, keepdims=True)
        idx = jnp.min(jnp.where(cand == m, iota, float(_NCAND_PAD)), axis=0, keepdims=True)
        sel = iota == idx
        fv.append(m)
        fa.append(jnp.sum(jnp.where(sel, ca, 0.0), axis=0, keepdims=True))
        fb.append(jnp.sum(jnp.where(sel, cb, 0.0), axis=0, keepdims=True))
        cand = jnp.where(sel, -jnp.inf, cand)
    fv = jnp.concatenate(fv, axis=0)
    e = jnp.exp(fv - fv[0:1])
    g = e / jnp.sum(e, axis=0, keepdims=True)
    to_int = lambda rows: jnp.concatenate(rows, axis=0).astype(jnp.int32)
    return to_int(fa), to_int(fb), g


_ID_OFF = 128.0
_TIE = 2 * _ID_OFF


def _col_max(x):
    tiles = [x[8 * i:8 * i + 8] for i in range(x.shape[0] // 8)]
    while len(tiles) > 1:
        tiles = [jnp.maximum(tiles[i], tiles[i + 1]) for i in range(0, len(tiles) - 1, 2)] + tiles[len(tiles) & ~1:]
    return jnp.max(tiles[0], axis=0, keepdims=True)


def _topk_unique(scores, ids_lhs, sub):
    zero = jnp.zeros((8, scores[0].shape[1]), F32)
    scores = list(scores)
    out = [([zero, zero], [zero, zero], zero) for _ in scores]
    for k in range(PEER_TOPK):
        here = sub == (k % 8)
        for j, s in enumerate(scores):
            v, r, worst = out[j]
            m = _col_max(s)
            e = s == m
            scores[j] = jnp.where(e, -jnp.inf, s)
            rk = _dot(ids_lhs, jnp.where(e, 1.0, 0.0).astype(BF16))[0:8]
            v[k // 8] = jnp.where(here, m, v[k // 8])
            r[k // 8] = jnp.where(here, rk, r[k // 8])
            out[j] = (v, r, jnp.maximum(worst, rk))
    return out


def _select_unique(v1, r1, v2, r2, ones_lhs, sub):
    w = v1[0].shape[1]
    bc = lambda x, row: jnp.broadcast_to(x[row:row + 1], (8, w))
    a1 = [r1[0] - _ID_OFF, r1[1] - _ID_OFF]
    v10, a10 = bc(v1[0], 0), bc(a1[0], 0)
    cand, ca, cb = [v10 + v2[0], v10 + v2[1]], [a10, a10], [r2[0], r2[1]]
    for k1 in range(1, 8):
        c = bc(v1[0], k1) + v2[0]
        if _ROW_LEN[k1] < 8:
            c = jnp.where(sub < _ROW_LEN[k1], c, -jnp.inf)
        cand.append(c)
        ca.append(bc(a1[0], k1))
        cb.append(r2[0])
    cand.append(v1[1] + bc(v2[0], 0))
    ca.append(a1[1])
    cb.append(bc(r2[0], 0))
    cand = jnp.concatenate(cand, axis=0)
    ca = jnp.concatenate(ca, axis=0)
    cb = jnp.concatenate(cb, axis=0)
    zero = jnp.zeros((8, w), F32)
    fv, fa, fb, worst = [zero, zero], [zero, zero], [zero, zero], zero
    for k in range(PEER_TOPK):
        m = _col_max(cand)
        e = cand == m
        cand = jnp.where(e, -jnp.inf, cand)
        ra = _dot(ones_lhs, jnp.where(e, ca, 0.0).astype(BF16))[0:8]
        rb = _dot(ones_lhs, jnp.where(e, cb, 0.0).astype(BF16))[0:8]
        worst = jnp.maximum(worst, rb)
        here = sub == (k % 8)
        fv[k // 8] = jnp.where(here, m, fv[k // 8])
        fa[k // 8] = jnp.where(here, ra, fa[k // 8])
        fb[k // 8] = jnp.where(here, rb, fb[k // 8])
    fv = jnp.concatenate(fv, axis=0)
    ex = jnp.exp(fv - fv[0:1])
    g = ex / jnp.sum(ex, axis=0, keepdims=True)
    a = jnp.concatenate(fa, axis=0).astype(jnp.int32)
    b = (jnp.concatenate(fb, axis=0) - _ID_OFF).astype(jnp.int32)
    return a, b, g, worst


def _route_kernel(x_ref, wq_ref, sk_ref, a_ref, b_ref, g_ref, s_ref, at_ref, bt_ref, gt_ref, *, tm):
    nch = tm // TILE
    sub = lax.broadcasted_iota(jnp.int32, (8, TILE), 0)
    ids_lhs = (lax.broadcasted_iota(jnp.int32, (16, PEER_KEYS), 1).astype(F32) + _ID_OFF).astype(BF16)
    ones_lhs = jnp.ones((16, 10 * 8), BF16)

    q = _dot(x_ref[...].astype(BF16), wq_ref[...]).astype(BF16)
    for hh in range(PEER_HEADS):
        for ch in range(nch):
            for c in range(2):
                col = (2 * hh + c) * PEER_KEYS
                s_ref[hh, 2 * ch + c] = _dot_nt(sk_ref[hh, c], q[ch * TILE:(ch + 1) * TILE, col:col + PEER_KEYS])

    def store(hh, ch, a, b, g):
        row = pl.multiple_of(hh * PEER_TOPK, PEER_TOPK)
        cols = slice(ch * TILE, (ch + 1) * TILE)
        at_ref[pl.ds(row, PEER_TOPK), cols] = a
        bt_ref[pl.ds(row, PEER_TOPK), cols] = b
        gt_ref[pl.ds(row, PEER_TOPK), cols] = g

    def head_unique(hh, worst):
        for ch in range(nch):
            (v1, r1, w1), (v2, r2, w2) = _topk_unique([s_ref[hh, 2 * ch], s_ref[hh, 2 * ch + 1]], ids_lhs, sub)
            a, b, g, w3 = _select_unique(v1, r1, v2, r2, ones_lhs, sub)
            store(hh, ch, a, b, g)
            worst = jnp.maximum(jnp.maximum(worst, w1), jnp.maximum(w2, w3))
        return worst

    worst = lax.fori_loop(0, PEER_HEADS, head_unique, jnp.zeros((8, TILE), F32))

    @pl.when(jnp.max(worst) >= _TIE)
    def _():
        def head_exact(hh, carry):
            for ch in range(nch):
                tops = [_topk_rows(s_ref[hh, 2 * ch + c], PEER_TOPK) for c in range(2)]
                store(hh, ch, *_route_select(tops[0][0], tops[0][1], tops[1][0], tops[1][1]))
            return carry

        lax.fori_loop(0, PEER_HEADS, head_exact, 0)

    for ch in range(nch):
        rows = slice(ch * TILE, (ch + 1) * TILE)
        a_ref[rows, :] = at_ref[:, rows].T
        b_ref[rows, :] = bt_ref[:, rows].T
        g_ref[rows, :] = gt_ref[:, rows].T


def _route(x, wq, sk, tm):
    t, d = x.shape
    nsel = PEER_HEADS * PEER_TOPK
    spec = pl.BlockSpec((tm, nsel), lambda i: (i, 0))
    return pl.pallas_call(
        functools.partial(_route_kernel, tm=tm),
        grid=(t // tm,),
        in_specs=[pl.BlockSpec((tm, d), lambda i: (i, 0)),
                  pl.BlockSpec(wq.shape, lambda i: (0, 0)),
                  pl.BlockSpec(sk.shape, lambda i: (0, 0, 0, 0))],
        out_specs=[spec, spec, spec],
        out_shape=[jax.ShapeDtypeStruct((t, nsel), jnp.int32),
                   jax.ShapeDtypeStruct((t, nsel), jnp.int32),
                   jax.ShapeDtypeStruct((t, nsel), F32)],
        scratch_shapes=[pltpu.VMEM((PEER_HEADS, 2 * (tm // TILE), PEER_KEYS, TILE), F32),
                        pltpu.VMEM((nsel, tm), jnp.int32), pltpu.VMEM((nsel, tm), jnp.int32),
                        pltpu.VMEM((nsel, tm), F32)],
        compiler_params=_params("parallel"),
        name="peer_route",
    )(x, wq, sk)


_WG_PITCH = PEER_KEYS + 8
_WG_GROUP = 16


def _wg_kernel(a_ref, b_ref, g_ref, o_ref, w_ref, *, tb):
    iota = lax.broadcasted_iota(jnp.int32, (PEER_KEYS, PEER_HEADS * PEER_TOPK), 0)
    for g0 in range(0, tb, _WG_GROUP):
        for t in range(g0, g0 + _WG_GROUP):
            pa = jnp.where(iota == a_ref[t:t + 1, :], 1.0, 0.0).astype(BF16)
            pb = jnp.where(iota == b_ref[t:t + 1, :], g_ref[t:t + 1, :], 0.0).astype(BF16)
            w_ref[t * _WG_PITCH:t * _WG_PITCH + PEER_KEYS, :] = _dot_nt(pa, pb)
        for a in range(PEER_KEYS):
            rows = w_ref[pl.ds(g0 * _WG_PITCH + a, _WG_GROUP, stride=_WG_PITCH), :]
            o_ref[g0:g0 + _WG_GROUP, a * PEER_KEYS:(a + 1) * PEER_KEYS] = rows.astype(BF16)


def _wg(a, b, g, tb):
    t, nsel = a.shape
    spec = pl.BlockSpec((tb, nsel), lambda i: (i, 0))
    return pl.pallas_call(
        functools.partial(_wg_kernel, tb=tb),
        grid=(t // tb,),
        in_specs=[spec, spec, spec],
        out_specs=pl.BlockSpec((tb, PEER_KEYS * PEER_KEYS), lambda i: (i, 0)),
        out_shape=jax.ShapeDtypeStruct((t, PEER_KEYS * PEER_KEYS), BF16),
        scratch_shapes=[pltpu.VMEM((tb * _WG_PITCH, PEER_KEYS), F32)],
        compiler_params=_params("parallel"),
        name="peer_wg",
    )(a, b, g)


def _experts_kernel(x_ref, wg_ref, u_ref, v_ref, g_ref, b_ref, o_ref, xb_ref, acc_ref, *, alpha):
    e = pl.program_id(1)

    @pl.when(e == 0)
    def _():
        xb_ref[...] = x_ref[...].astype(BF16)
        acc_ref[...] = jnp.zeros_like(acc_ref)

    act = jax.nn.gelu(_dot_nt(xb_ref[...], u_ref[...]))
    p = (act * wg_ref[...].astype(F32)).astype(BF16)
    acc_ref[...] += _dot(p, v_ref[...])

    @pl.when(e == pl.num_programs(1) - 1)
    def _():
        o_ref[...] = _ln(alpha * x_ref[...] + acc_ref[...], g_ref[...], b_ref[...])


def _experts(x, wg, u, v, g, b, alpha, tm, eb):
    t, d = x.shape
    ne = u.shape[0]
    return pl.pallas_call(
        functools.partial(_experts_kernel, alpha=alpha),
        grid=(t // tm, ne // eb),
        in_specs=[pl.BlockSpec((tm, d), lambda i, e: (i, 0)),
                  pl.BlockSpec((tm, eb), lambda i, e: (i, e)),
                  pl.BlockSpec((eb, d), lambda i, e: (e, 0)),
                  pl.BlockSpec((eb, d), lambda i, e: (e, 0)),
                  pl.BlockSpec((1, d), lambda i, e: (0, 0)),
                  pl.BlockSpec((1, d), lambda i, e: (0, 0))],
        out_specs=pl.BlockSpec((tm, d), lambda i, e: (i, 0)),
        out_shape=jax.ShapeDtypeStruct((t, d), F32),
        scratch_shapes=[pltpu.VMEM((tm, d), BF16), pltpu.VMEM((tm, d), F32)],
        compiler_params=_params("parallel", "arbitrary"),
        name="peer_experts",
    )(x, wg, u, v, g.reshape(1, d), b.reshape(1, d))


def _layer(x, nseq, seqlen, pos0, mem_k, mem_v, s0, lw, alpha):
    t, d = x.shape
    tm = min(512, t)
    h = _mm(x, lw["w_in"], tm, lw["w_in"].shape[1])
    mix, s_new, van = _mixer(h, nseq, seqlen, pos0, s0, lw["w_s"], lw["b_s"], lw["gate_ln_g"],
                             lw["gate_ln_b"], lw["ret_gn_g"], lw["ret_gn_b"])
    x = _mm_res_ln(mix, lw["w_o"], x, lw["ln1_g"], lw["ln1_b"], alpha, tm)
    qc = _mm(x, lw["ca_wq"], tm, 1024)
    ca = _attn(qc, mem_k, mem_v, nseq, seqlen)
    x = _mm_res_ln(ca, lw["ca_wo"], x, lw["ln2_g"], lw["ln2_b"], alpha, tm)
    a, b, g = _route(x, lw["peer_wq"], lw["peer_subkeys"], min(512, t))
    wg = _wg(a, b, g, 64)
    x = _experts(x, wg, lw["peer_u"], lw["peer_v"], lw["ln3_g"], lw["ln3_b"], alpha, tm, 1024)
    return x, s_new, van


def kernel(x_prompt, x_sample, mem_prompt, cache_mem_k, cache_mem_v, state_ret, w_in, w_s, b_s, gate_ln_g, gate_ln_b, ret_gn_g, ret_gn_b, w_o, ln1_g, ln1_b, ca_wq, ca_wk, ca_wv, ca_wo, ln2_g, ln2_b, peer_wq, peer_subkeys, peer_u, peer_v, ln3_g, ln3_b):
    depth = w_in.shape[0]
    bp, lp, d = x_prompt.shape
    bs, ls, _ = x_sample.shape
    alpha = (2 * depth) ** 0.25
    yp = x_prompt.reshape(bp * lp, d)
    ys = x_sample.reshape(bs * ls, d)
    mem2d = mem_prompt.reshape(bp * N_MEM, d)
    past_len = PAST_LEN
    mk_l, mv_l, sp_l, ss_l, gv_l = [], [], [], [], []
    for l in range(depth):
        lw = {
            "w_in": w_in[l].astype(BF16), "w_s": w_s[l], "b_s": b_s[l],
            "gate_ln_g": gate_ln_g[l], "gate_ln_b": gate_ln_b[l],
            "ret_gn_g": ret_gn_g[l], "ret_gn_b": ret_gn_b[l],
            "w_o": w_o[l].astype(BF16), "ln1_g": ln1_g[l], "ln1_b": ln1_b[l],
            "ca_wq": ca_wq[l].astype(BF16), "ca_wo": ca_wo[l].astype(BF16),
            "ln2_g": ln2_g[l], "ln2_b": ln2_b[l],
            "peer_wq": peer_wq[l].astype(BF16),
            "peer_subkeys": peer_subkeys[l].astype(BF16),
            "peer_u": peer_u[l].astype(BF16), "peer_v": peer_v[l].astype(BF16),
            "ln3_g": ln3_g[l], "ln3_b": ln3_b[l],
        }
        mem_k = _mm(mem2d, ca_wk[l].astype(BF16), 512, 1024)
        mem_v = _mm(mem2d, ca_wv[l].astype(BF16), 512, 1024)
        yp, sp, _ = _layer(yp, bp, lp, 0, mem_k.reshape(bp, N_MEM, d), mem_v.reshape(bp, N_MEM, d),
                           None, lw, alpha)
        ys, ss, gv = _layer(ys, bs, ls, past_len, cache_mem_k[l].reshape(bs, N_MEM, d),
                            cache_mem_v[l].reshape(bs, N_MEM, d), state_ret[l], lw, alpha)
        mk_l.append(mem_k.reshape(bp, N_MEM, CA_HEADS, CA_DH))
        mv_l.append(mem_v.reshape(bp, N_MEM, CA_HEADS, CA_DH))
        sp_l.append(sp)
        ss_l.append(ss)
        gv_l.append(gv.reshape(bs, ls, A_GROUPS, A_CH))
    return (yp.reshape(bp, lp, d), ys.reshape(bs, ls, d), jnp.stack(mk_l), jnp.stack(mv_l),
            jnp.stack(sp_l), jnp.stack(ss_l), jnp.stack(gv_l))
```

```python
import functools

import jax
import jax.numpy as jnp
from jax import lax
from jax.experimental import pallas as pl
from jax.experimental.pallas import tpu as pltpu

F32 = jnp.float32
BF16 = jnp.bfloat16

A_GROUPS = 4
A_CH = 128
CHUNK = 128
R_HEADS = 4
R_DK = 128
ROPE_BASE = 10000.0
N_MEM = 256
CA_HEADS = 4
CA_DH = 256
PEER_HEADS = 8
PEER_KEYS = 128
PEER_TOPK = 16
LN_EPS = 1e-5
PAST_LEN = 16384
TILE = 128
VMEM_LIMIT = 48 * 1024 * 1024


def _params(*sem):
    return pltpu.CompilerParams(dimension_semantics=sem, vmem_limit_bytes=VMEM_LIMIT)


def _ln(x, g, b):
    mu = jnp.mean(x, axis=-1, keepdims=True)
    xc = x - mu
    var = jnp.mean(xc * xc, axis=-1, keepdims=True)
    return xc * lax.rsqrt(var + LN_EPS) * g + b


def _dot(a, b):
    return jnp.dot(a, b, preferred_element_type=F32)


def _dot_nt(a, b):
    return lax.dot_general(a, b, (((1,), (1,)), ((), ())), preferred_element_type=F32)


def _mm_kernel(x_ref, w_ref, o_ref):
    o_ref[...] = _dot(x_ref[...].astype(BF16), w_ref[...])


def _mm(x, w, tm, tn):
    t, k = x.shape
    n = w.shape[1]
    return pl.pallas_call(
        _mm_kernel,
        grid=(t // tm, n // tn),
        in_specs=[pl.BlockSpec((tm, k), lambda i, j: (i, 0)),
                  pl.BlockSpec((k, tn), lambda i, j: (0, j))],
        out_specs=pl.BlockSpec((tm, tn), lambda i, j: (i, j)),
        out_shape=jax.ShapeDtypeStruct((t, n), F32),
        compiler_params=_params("parallel", "arbitrary"),
        name="mm",
    )(x, w)


def _mm_res_ln_kernel(a_ref, w_ref, x_ref, g_ref, b_ref, o_ref, *, alpha):
    y = _dot(a_ref[...], w_ref[...])
    o_ref[...] = _ln(alpha * x_ref[...] + y, g_ref[...], b_ref[...])


def _mm_res_ln(a, w, x, g, b, alpha, tm):
    t, k = a.shape
    d = w.shape[1]
    return pl.pallas_call(
        functools.partial(_mm_res_ln_kernel, alpha=alpha),
        grid=(t // tm,),
        in_specs=[pl.BlockSpec((tm, k), lambda i: (i, 0)),
                  pl.BlockSpec((k, d), lambda i: (0, 0)),
                  pl.BlockSpec((tm, d), lambda i: (i, 0)),
                  pl.BlockSpec((1, d), lambda i: (0, 0)),
                  pl.BlockSpec((1, d), lambda i: (0, 0))],
        out_specs=pl.BlockSpec((tm, d), lambda i: (i, 0)),
        out_shape=jax.ShapeDtypeStruct((t, d), F32),
        compiler_params=_params("parallel"),
        name="mm_res_ln",
    )(a, w, x, g.reshape(1, d), b.reshape(1, d))


def _mixer_consts(lc, pos):
    half = R_DK // 2
    inv = 1.0 / (ROPE_BASE ** (jnp.arange(half, dtype=F32) / half))
    ang = pos.astype(F32)[:, None] * inv[None, :]
    cos, sin = jnp.cos(ang), jnp.sin(ang)
    rot_c = jnp.concatenate([cos, cos], axis=-1)
    rot_s = jnp.concatenate([-sin, sin], axis=-1)
    lg = jnp.log(1.0 - 2.0 ** (-5.0 - jnp.arange(R_HEADS, dtype=F32)))
    r = jnp.arange(TILE)
    i = (r % lc).astype(F32)
    same = (r[:, None] // lc) == (r[None, :] // lc)
    diff = i[:, None] - i[None, :]
    dm = jnp.where(same[None] & (diff[None] >= 0),
                   jnp.exp(jnp.maximum(diff, 0.0)[None] * lg[:, None, None]), 0.0)
    ones = jnp.ones((1, 1, TILE), F32)
    qd = jnp.exp((i + 1.0)[None, :] * lg[:, None])[:, :, None] * ones
    kd = jnp.exp((lc - 1.0 - i)[None, :] * lg[:, None])[:, :, None] * ones
    cd = jnp.exp(lc * lg)[:, None, None] * jnp.ones((1, 8, TILE), F32)
    return rot_c, rot_s, dm, qd, kd, cd


def _rot(x, c, s):
    return x * c + pltpu.roll(x, R_DK // 2, 1) * s


def _gate_part(h_ref, ws_ref, bsb_ref, glg_ref, glb_ref, mix_ref, van_ref, lc):
    r = lax.broadcasted_iota(jnp.int32, (TILE, TILE), 0)
    c = lax.broadcasted_iota(jnp.int32, (TILE, TILE), 1)
    mask = (c <= r) & ((r // lc) == (c // lc))
    half = A_GROUPS * A_CH
    for g in range(A_GROUPS):
        sl = slice(g * A_CH, (g + 1) * A_CH)
        u = jax.nn.gelu(h_ref[:, sl])
        v = jax.nn.gelu(h_ref[:, half + g * A_CH: half + (g + 1) * A_CH])
        vn = _ln(v, glg_ref[g:g + 1, :], glb_ref[g:g + 1, :])
        w = jnp.where(mask, ws_ref[g], 0.0).astype(BF16)
        mixed = _dot(w, vn.astype(BF16)) + bsb_ref[g]
        mix_ref[:, sl] = (u * mixed).astype(BF16)
        if van_ref is not None:
            van_ref[:, sl] = vn


def _ret_head_inputs(h_ref, hd, rc, rs):
    base = 2 * A_GROUPS * A_CH
    w = R_HEADS * R_DK
    sl = lambda j: slice(base + j * w + hd * R_DK, base + j * w + (hd + 1) * R_DK)
    q = _rot(h_ref[:, sl(0)], rc, rs)
    k = _rot(h_ref[:, sl(1)], rc, rs) * (R_DK ** -0.5)
    return q, k, h_ref[:, sl(2)], h_ref[:, sl(3)]


def _ret_finish(o, gr, gng_ref, gnb_ref, mix_ref, hd):
    on = _ln(o, gng_ref[hd:hd + 1, :], gnb_ref[hd:hd + 1, :])
    off = A_GROUPS * A_CH + hd * R_DK
    mix_ref[:, off:off + R_DK] = (jax.nn.silu(gr) * on).astype(BF16)


def _mixer_prompt_kernel(h_ref, rc_ref, rs_ref, ws_ref, bsb_ref, glg_ref, glb_ref, dm_ref, qd_ref,
                         kd_ref, cd_ref, gng_ref, gnb_ref, mix_ref, s_ref):
    @pl.when(pl.program_id(1) == 0)
    def _():
        s_ref[...] = jnp.zeros_like(s_ref)

    _gate_part(h_ref, ws_ref, bsb_ref, glg_ref, glb_ref, mix_ref, None, CHUNK)
    rc, rs = rc_ref[...], rs_ref[...]
    for hd in range(R_HEADS):
        q, k, v, gr = _ret_head_inputs(h_ref, hd, rc, rs)
        qb, kb, vb = q.astype(BF16), k.astype(BF16), v.astype(BF16)
        inner = _dot_nt(qb, kb) * dm_ref[hd]
        s = s_ref[0, hd]
        o = _dot(inner.astype(BF16), vb) + _dot(qb, s.astype(BF16)) * qd_ref[hd]
        kt = (k * kd_ref[hd]).T.astype(BF16)
        s_ref[0, hd] = s * cd_ref[hd, 0:1, :] + _dot(kt, vb)
        _ret_finish(o, gr, gng_ref, gnb_ref, mix_ref, hd)


def _mixer_sample_kernel(h_ref, rc_ref, rs_ref, ws_ref, bsb_ref, glg_ref, glb_ref, dm_ref, qd_ref,
                         kd_ref, cd_ref, gng_ref, gnb_ref, s0_ref, mix_ref, s_ref, van_ref, *, lc):
    nb = TILE // lc
    _gate_part(h_ref, ws_ref, bsb_ref, glg_ref, glb_ref, mix_ref, van_ref, lc)
    rc, rs = rc_ref[...], rs_ref[...]
    rb = lax.broadcasted_iota(jnp.int32, (TILE, R_DK), 0) // lc
    for hd in range(R_HEADS):
        q, k, v, gr = _ret_head_inputs(h_ref, hd, rc, rs)
        qb, kb, vb = q.astype(BF16), k.astype(BF16), v.astype(BF16)
        inner = _dot_nt(qb, kb) * dm_ref[hd]
        cross = jnp.concatenate(
            [_dot(qb[bi * lc:(bi + 1) * lc], s0_ref[bi, hd].astype(BF16)) for bi in range(nb)], axis=0)
        o = _dot(inner.astype(BF16), vb) + cross * qd_ref[hd]
        kt = (k * kd_ref[hd]).T.astype(BF16)
        cd = cd_ref[hd, 0:1, :]
        for bi in range(nb):
            vm = jnp.where(rb == bi, v, 0.0).astype(BF16)
            s_ref[bi, hd] = s0_ref[bi, hd] * cd + _dot(kt, vm)
        _ret_finish(o, gr, gng_ref, gnb_ref, mix_ref, hd)


def _mixer(h, nseq, seqlen, pos0, s0, w_s, b_s, glg, glb, gng, gnb):
    t, win = h.shape
    dmix = A_GROUPS * A_CH + R_HEADS * R_DK
    if seqlen % CHUNK == 0:
        lc, nchunk = CHUNK, seqlen // CHUNK
        pos = pos0 + jnp.arange(seqlen, dtype=jnp.int32)
    else:
        assert TILE % seqlen == 0 and nseq % (TILE // seqlen) == 0 and s0 is not None
        lc = seqlen
        pos = pos0 + (jnp.arange(TILE, dtype=jnp.int32) % lc)
    rot_c, rot_s, dm, qd, kd, cd = _mixer_consts(lc, pos)
    reps = TILE // lc
    ws_t = jnp.tile(w_s[:, :lc, :lc], (1, reps, reps))
    bsb = jnp.tile(b_s[:, :lc], (1, reps))[:, :, None] * jnp.ones((1, 1, A_CH), F32)
    full = lambda shape: pl.BlockSpec(shape, lambda *_: (0,) * len(shape))
    const_specs = [full((A_GROUPS, TILE, TILE)), full((A_GROUPS, TILE, A_CH)),
                   full((A_GROUPS, A_CH)), full((A_GROUPS, A_CH)),
                   full((R_HEADS, TILE, TILE)), full((R_HEADS, TILE, R_DK)),
                   full((R_HEADS, TILE, R_DK)), full((R_HEADS, 8, R_DK)),
                   full((R_HEADS, R_DK)), full((R_HEADS, R_DK))]
    consts = (ws_t, bsb, glg, glb, dm, qd, kd, cd, gng, gnb)
    if lc == CHUNK:
        mix, s_new = pl.pallas_call(
            _mixer_prompt_kernel,
            grid=(nseq, nchunk),
            in_specs=[pl.BlockSpec((TILE, win), lambda b, n: (b * nchunk + n, 0)),
                      pl.BlockSpec((TILE, R_DK), lambda b, n: (n, 0)),
                      pl.BlockSpec((TILE, R_DK), lambda b, n: (n, 0))] + const_specs,
            out_specs=[pl.BlockSpec((TILE, dmix), lambda b, n: (b * nchunk + n, 0)),
                       pl.BlockSpec((1, R_HEADS, R_DK, R_DK), lambda b, n: (b, 0, 0, 0))],
            out_shape=[jax.ShapeDtypeStruct((t, dmix), BF16),
                       jax.ShapeDtypeStruct((nseq, R_HEADS, R_DK, R_DK), F32)],
            compiler_params=_params("parallel", "arbitrary"),
            name="mixer_prompt",
        )(h, rot_c, rot_s, *consts)
        return mix, s_new, None
    mix, s_new, van = pl.pallas_call(
        functools.partial(_mixer_sample_kernel, lc=lc),
        grid=(t // TILE,),
        in_specs=[pl.BlockSpec((TILE, win), lambda i: (i, 0)),
                  full((TILE, R_DK)), full((TILE, R_DK))] + const_specs
                 + [pl.BlockSpec((reps, R_HEADS, R_DK, R_DK), lambda i: (i, 0, 0, 0))],
        out_specs=[pl.BlockSpec((TILE, dmix), lambda i: (i, 0)),
                   pl.BlockSpec((reps, R_HEADS, R_DK, R_DK), lambda i: (i, 0, 0, 0)),
                   pl.BlockSpec((TILE, A_GROUPS * A_CH), lambda i: (i, 0))],
        out_shape=[jax.ShapeDtypeStruct((t, dmix), BF16),
                   jax.ShapeDtypeStruct((nseq, R_HEADS, R_DK, R_DK), F32),
                   jax.ShapeDtypeStruct((t, A_GROUPS * A_CH), F32)],
        compiler_params=_params("parallel"),
        name="mixer_sample",
    )(h, rot_c, rot_s, *consts, s0)
    return mix, s_new, van


def _attn_kernel(q_ref, k_ref, v_ref, o_ref, *, nb, rows):
    split = len(k_ref.shape) == 4
    for bi in range(nb):
        rs = slice(bi * rows, (bi + 1) * rows)
        for hh in range(CA_HEADS):
            sl = slice(hh * CA_DH, (hh + 1) * CA_DH)
            kh = k_ref[bi, :, hh, :] if split else k_ref[bi, :, sl]
            vh = v_ref[bi, :, hh, :] if split else v_ref[bi, :, sl]
            sc = _dot_nt(q_ref[rs, sl].astype(BF16), kh.astype(BF16)) * (CA_DH ** -0.5)
            e = jnp.exp(sc - jnp.max(sc, axis=-1, keepdims=True))
            p = e / jnp.sum(e, axis=-1, keepdims=True)
            o_ref[rs, sl] = _dot(p.astype(BF16), vh.astype(BF16)).astype(BF16)


def _attn(qc, mem_k, mem_v, nseq, seqlen):
    t, d = qc.shape
    kv_tail = mem_k.shape[1:]
    zeros = (0,) * len(kv_tail)
    if seqlen >= 512:
        nb, rows = 1, 512
        per_seq = seqlen // rows
        grid = (nseq, per_seq)
        q_map = lambda b, i: (b * per_seq + i, 0)
        kv_map = lambda b, i: (b,) + zeros
        sem = ("parallel", "arbitrary")
    else:
        nb, rows = 4, seqlen
        grid = (nseq // nb,)
        q_map = lambda i: (i, 0)
        kv_map = lambda i: (i,) + zeros
        sem = ("parallel",)
    return pl.pallas_call(
        functools.partial(_attn_kernel, nb=nb, rows=rows),
        grid=grid,
        in_specs=[pl.BlockSpec((nb * rows, d), q_map),
                  pl.BlockSpec((nb,) + kv_tail, kv_map),
                  pl.BlockSpec((nb,) + kv_tail, kv_map)],
        out_specs=pl.BlockSpec((nb * rows, d), q_map),
        out_shape=jax.ShapeDtypeStruct((t, d), BF16),
        compiler_params=_params(*sem),
        name="attn",
    )(qc, mem_k, mem_v)


def _ca_block_kernel(x_ref, k_ref, v_ref, wq_ref, wo_ref, g_ref, b_ref, o_ref, cat_ref, *, alpha):
    x = x_ref[...]
    q = _dot(x.astype(BF16), wq_ref[...])
    for hh in range(CA_HEADS):
        sl = slice(hh * CA_DH, (hh + 1) * CA_DH)
        sc = _dot_nt(q[:, sl].astype(BF16), k_ref[0, :, sl].astype(BF16)) * (CA_DH ** -0.5)
        e = jnp.exp(sc - jnp.max(sc, axis=-1, keepdims=True))
        p = e / jnp.sum(e, axis=-1, keepdims=True)
        cat_ref[:, sl] = _dot(p.astype(BF16), v_ref[0, :, sl].astype(BF16)).astype(BF16)
    o_ref[...] = _ln(alpha * x + _dot(cat_ref[...], wo_ref[...]), g_ref[...], b_ref[...])


def _ca_block(x, mem_k, mem_v, wq, wo, g, b, alpha, nseq, seqlen, rows):
    t, d = x.shape
    per_seq = seqlen // rows
    row_map = lambda s, i: (s * per_seq + i, 0)
    kv_spec = pl.BlockSpec((1, N_MEM, d), lambda s, i: (s, 0, 0))
    w_spec = pl.BlockSpec((d, d), lambda s, i: (0, 0))
    vec_spec = pl.BlockSpec((1, d), lambda s, i: (0, 0))
    return pl.pallas_call(
        functools.partial(_ca_block_kernel, alpha=alpha),
        grid=(nseq, per_seq),
        in_specs=[pl.BlockSpec((rows, d), row_map), kv_spec, kv_spec, w_spec, w_spec, vec_spec, vec_spec],
        out_specs=pl.BlockSpec((rows, d), row_map),
        out_shape=jax.ShapeDtypeStruct((t, d), F32),
        scratch_shapes=[pltpu.VMEM((rows, d), BF16)],
        compiler_params=_params("parallel", "arbitrary"),
        name="ca_block",
    )(x, mem_k, mem_v, wq, wo, g.reshape(1, d), b.reshape(1, d))


_ROW_LEN = [PEER_TOPK // (k1 + 1) for k1 in range(PEER_TOPK)]
_NCAND = sum(_ROW_LEN)
_NCAND_PAD = -(-_NCAND // 8) * 8


def _topk_rows(s, k):
    n = s.shape[0]
    iota = lax.broadcasted_iota(jnp.int32, s.shape, 0).astype(F32)
    vals, idxs = [], []
    for _ in range(k):
        m = jnp.max(s, axis=0, keepdims=True)
        idx = jnp.min(jnp.where(s == m, iota, float(n)), axis=0, keepdims=True)
        vals.append(m)
        idxs.append(idx)
        s = jnp.where(iota == idx, -jnp.inf, s)
    return vals, idxs


def _route_select(sv1, si1, sv2, si2):
    w = sv1[0].shape[1]
    v2 = jnp.concatenate(sv2, axis=0)
    i2 = jnp.concatenate(si2, axis=0)
    cv, ca, cb = [], [], []
    for k1, n in enumerate(_ROW_LEN):
        cv.append(sv1[k1] + v2[0:n])
        ca.append(jnp.broadcast_to(si1[k1], (n, w)))
        cb.append(i2[0:n])
    pad = _NCAND_PAD - _NCAND
    if pad:
        cv.append(jnp.full((pad, w), -jnp.inf, F32))
        ca.append(jnp.zeros((pad, w), F32))
        cb.append(jnp.zeros((pad, w), F32))
    cand = jnp.concatenate(cv, axis=0)
    ca = jnp.concatenate(ca, axis=0)
    cb = jnp.concatenate(cb, axis=0)
    iota = lax.broadcasted_iota(jnp.int32, cand.shape, 0).astype(F32)
    fv, fa, fb = [], [], []
    for _ in range(PEER_TOPK):
        m = jnp.max(cand, axis=0, keepdims=True)
        idx = jnp.min(jnp.where(cand == m, iota, float(_NCAND_PAD)), axis=0, keepdims=True)
        sel = iota == idx
        fv.append(m)
        fa.append(jnp.sum(jnp.where(sel, ca, 0.0), axis=0, keepdims=True))
        fb.append(jnp.sum(jnp.where(sel, cb, 0.0), axis=0, keepdims=True))
        cand = jnp.where(sel, -jnp.inf, cand)
    fv = jnp.concatenate(fv, axis=0)
    e = jnp.exp(fv - fv[0:1])
    g = e / jnp.sum(e, axis=0, keepdims=True)
    to_int = lambda rows: jnp.concatenate(rows, axis=0).astype(jnp.int32)
    return to_int(fa), to_int(fb), g


_ID_OFF = 128.0
_TIE = 2 * _ID_OFF


def _col_max(x):
    tiles = [x[8 * i:8 * i + 8] for i in range(x.shape[0] // 8)]
    while len(tiles) > 1:
        tiles = [jnp.maximum(tiles[i], tiles[i + 1]) for i in range(0, len(tiles) - 1, 2)] + tiles[len(tiles) & ~1:]
    return jnp.max(tiles[0], axis=0, keepdims=True)


def _topk_unique(scores, ids_lhs, sub):
    zero = jnp.zeros((8, scores[0].shape[1]), F32)
    scores = list(scores)
    out = [([zero, zero], [zero, zero], zero) for _ in scores]
    for k in range(PEER_TOPK):
        here = sub == (k % 8)
        for j, s in enumerate(scores):
            v, r, worst = out[j]
            m = _col_max(s)
            e = s == m
            scores[j] = jnp.where(e, -jnp.inf, s)
            rk = _dot(ids_lhs, jnp.where(e, 1.0, 0.0).astype(BF16))[0:8]
            v[k // 8] = jnp.where(here, m, v[k // 8])
            r[k // 8] = jnp.where(here, rk, r[k // 8])
            out[j] = (v, r, jnp.maximum(worst, rk))
    return out


def _select_unique(v1, r1, v2, r2, ones_lhs, sub):
    w = v1[0].shape[1]
    bc = lambda x, row: jnp.broadcast_to(x[row:row + 1], (8, w))
    a1 = [r1[0] - _ID_OFF, r1[1] - _ID_OFF]
    v10, a10 = bc(v1[0], 0), bc(a1[0], 0)
    cand, ca, cb = [v10 + v2[0], v10 + v2[1]], [a10, a10], [r2[0], r2[1]]
    for k1 in range(1, 8):
        c = bc(v1[0], k1) + v2[0]
        if _ROW_LEN[k1] < 8:
            c = jnp.where(sub < _ROW_LEN[k1], c, -jnp.inf)
        cand.append(c)
        ca.append(bc(a1[0], k1))
        cb.append(r2[0])
    cand.append(v1[1] + bc(v2[0], 0))
    ca.append(a1[1])
    cb.append(bc(r2[0], 0))
    cand = jnp.concatenate(cand, axis=0)
    ca = jnp.concatenate(ca, axis=0)
    cb = jnp.concatenate(cb, axis=0)
    zero = jnp.zeros((8, w), F32)
    fv, fa, fb, worst = [zero, zero], [zero, zero], [zero, zero], zero
    for k in range(PEER_TOPK):
        m = _col_max(cand)
        e = cand == m
        cand = jnp.where(e, -jnp.inf, cand)
        ra = _dot(ones_lhs, jnp.where(e, ca, 0.0).astype(BF16))[0:8]
        rb = _dot(ones_lhs, jnp.where(e, cb, 0.0).astype(BF16))[0:8]
        worst = jnp.maximum(worst, rb)
        here = sub == (k % 8)
        fv[k // 8] = jnp.where(here, m, fv[k // 8])
        fa[k // 8] = jnp.where(here, ra, fa[k // 8])
        fb[k // 8] = jnp.where(here, rb, fb[k // 8])
    fv = jnp.concatenate(fv, axis=0)
    ex = jnp.exp(fv - fv[0:1])
    g = ex / jnp.sum(ex, axis=0, keepdims=True)
    a = jnp.concatenate(fa, axis=0).astype(jnp.int32)
    b = (jnp.concatenate(fb, axis=0) - _ID_OFF).astype(jnp.int32)
    return a, b, g, worst


def _route_kernel(x_ref, wq_ref, sk_ref, a_ref, b_ref, g_ref, s_ref, at_ref, bt_ref, gt_ref, *, tm):
    nch = tm // TILE
    sub = lax.broadcasted_iota(jnp.int32, (8, TILE), 0)
    ids_lhs = (lax.broadcasted_iota(jnp.int32, (16, PEER_KEYS), 1).astype(F32) + _ID_OFF).astype(BF16)
    ones_lhs = jnp.ones((16, 10 * 8), BF16)

    q = _dot(x_ref[...].astype(BF16), wq_ref[...]).astype(BF16)
    for hh in range(PEER_HEADS):
        for ch in range(nch):
            for c in range(2):
                col = (2 * hh + c) * PEER_KEYS
                s_ref[hh, 2 * ch + c] = _dot_nt(sk_ref[hh, c], q[ch * TILE:(ch + 1) * TILE, col:col + PEER_KEYS])

    def store(hh, ch, a, b, g):
        row = pl.multiple_of(hh * PEER_TOPK, PEER_TOPK)
        cols = slice(ch * TILE, (ch + 1) * TILE)
        at_ref[pl.ds(row, PEER_TOPK), cols] = a
        bt_ref[pl.ds(row, PEER_TOPK), cols] = b
        gt_ref[pl.ds(row, PEER_TOPK), cols] = g

    def head_unique(hh, worst):
        for ch in range(nch):
            (v1, r1, w1), (v2, r2, w2) = _topk_unique([s_ref[hh, 2 * ch], s_ref[hh, 2 * ch + 1]], ids_lhs, sub)
            a, b, g, w3 = _select_unique(v1, r1, v2, r2, ones_lhs, sub)
            store(hh, ch, a, b, g)
            worst = jnp.maximum(jnp.maximum(worst, w1), jnp.maximum(w2, w3))
        return worst

    worst = lax.fori_loop(0, PEER_HEADS, head_unique, jnp.zeros((8, TILE), F32))

    @pl.when(jnp.max(worst) >= _TIE)
    def _():
        def head_exact(hh, carry):
            for ch in range(nch):
                tops = [_topk_rows(s_ref[hh, 2 * ch + c], PEER_TOPK) for c in range(2)]
                store(hh, ch, *_route_select(tops[0][0], tops[0][1], tops[1][0], tops[1][1]))
            return carry

        lax.fori_loop(0, PEER_HEADS, head_exact, 0)

    for ch in range(nch):
        rows = slice(ch * TILE, (ch + 1) * TILE)
        a_ref[rows, :] = at_ref[:, rows].T
        b_ref[rows, :] = bt_ref[:, rows].T
        g_ref[rows, :] = gt_ref[:, rows].T


def _route(x, wq, sk, tm):
    t, d = x.shape
    nsel = PEER_HEADS * PEER_TOPK
    spec = pl.BlockSpec((tm, nsel), lambda i: (i, 0))
    return pl.pallas_call(
        functools.partial(_route_kernel, tm=tm),
        grid=(t // tm,),
        in_specs=[pl.BlockSpec((tm, d), lambda i: (i, 0)),
                  pl.BlockSpec(wq.shape, lambda i: (0, 0)),
                  pl.BlockSpec(sk.shape, lambda i: (0, 0, 0, 0))],
        out_specs=[spec, spec, spec],
        out_shape=[jax.ShapeDtypeStruct((t, nsel), jnp.int32),
                   jax.ShapeDtypeStruct((t, nsel), jnp.int32),
                   jax.ShapeDtypeStruct((t, nsel), F32)],
        scratch_shapes=[pltpu.VMEM((PEER_HEADS, 2 * (tm // TILE), PEER_KEYS, TILE), F32),
                        pltpu.VMEM((nsel, tm), jnp.int32), pltpu.VMEM((nsel, tm), jnp.int32),
                        pltpu.VMEM((nsel, tm), F32)],
        compiler_params=_params("parallel"),
        name="peer_route",
    )(x, wq, sk)


_WG_PITCH = PEER_KEYS + 8
_WG_GROUP = 16


def _wg_kernel(a_ref, b_ref, g_ref, o_ref, w_ref, *, tb):
    iota = lax.broadcasted_iota(jnp.int32, (PEER_KEYS, PEER_HEADS * PEER_TOPK), 0)
    for g0 in range(0, tb, _WG_GROUP):
        for t in range(g0, g0 + _WG_GROUP):
            pa = jnp.where(iota == a_ref[t:t + 1, :], 1.0, 0.0).astype(BF16)
            pb = jnp.where(iota == b_ref[t:t + 1, :], g_ref[t:t + 1, :], 0.0).astype(BF16)
            w_ref[t * _WG_PITCH:t * _WG_PITCH + PEER_KEYS, :] = _dot_nt(pa, pb)
        for a in range(PEER_KEYS):
            rows = w_ref[pl.ds(g0 * _WG_PITCH + a, _WG_GROUP, stride=_WG_PITCH), :]
            o_ref[g0:g0 + _WG_GROUP, a * PEER_KEYS:(a + 1) * PEER_KEYS] = rows.astype(BF16)


def _wg(a, b, g, tb):
    t, nsel = a.shape
    spec = pl.BlockSpec((tb, nsel), lambda i: (i, 0))
    return pl.pallas_call(
        functools.partial(_wg_kernel, tb=tb),
        grid=(t // tb,),
        in_specs=[spec, spec, spec],
        out_specs=pl.BlockSpec((tb, PEER_KEYS * PEER_KEYS), lambda i: (i, 0)),
        out_shape=jax.ShapeDtypeStruct((t, PEER_KEYS * PEER_KEYS), BF16),
        scratch_shapes=[pltpu.VMEM((tb * _WG_PITCH, PEER_KEYS), F32)],
        compiler_params=_params("parallel"),
        name="peer_wg",
    )(a, b, g)


def _experts_kernel(x_ref, wg_ref, u_ref, v_ref, g_ref, b_ref, o_ref, xb_ref, acc_ref, *, alpha):
    e = pl.program_id(1)

    @pl.when(e == 0)
    def _():
        xb_ref[...] = x_ref[...].astype(BF16)
        acc_ref[...] = jnp.zeros_like(acc_ref)

    act = jax.nn.gelu(_dot_nt(xb_ref[...], u_ref[...]))
    p = (act * wg_ref[...].astype(F32)).astype(BF16)
    acc_ref[...] += _dot(p, v_ref[...])

    @pl.when(e == pl.num_programs(1) - 1)
    def _():
        o_ref[...] = _ln(alpha * x_ref[...] + acc_ref[...], g_ref[...], b_ref[...])


def _experts(x, wg, u, v, g, b, alpha, tm, eb):
    t, d = x.shape
    ne = u.shape[0]
    return pl.pallas_call(
        functools.partial(_experts_kernel, alpha=alpha),
        grid=(t // tm, ne // eb),
        in_specs=[pl.BlockSpec((tm, d), lambda i, e: (i, 0)),
                  pl.BlockSpec((tm, eb), lambda i, e: (i, e)),
                  pl.BlockSpec((eb, d), lambda i, e: (e, 0)),
                  pl.BlockSpec((eb, d), lambda i, e: (e, 0)),
                  pl.BlockSpec((1, d), lambda i, e: (0, 0)),
                  pl.BlockSpec((1, d), lambda i, e: (0, 0))],
        out_specs=pl.BlockSpec((tm, d), lambda i, e: (i, 0)),
        out_shape=jax.ShapeDtypeStruct((t, d), F32),
        scratch_shapes=[pltpu.VMEM((tm, d), BF16), pltpu.VMEM((tm, d), F32)],
        compiler_params=_params("parallel", "arbitrary"),
        name="peer_experts",
    )(x, wg, u, v, g.reshape(1, d), b.reshape(1, d))


def _layer(x, nseq, seqlen, pos0, mem_k, mem_v, s0, lw, alpha):
    t, d = x.shape
    tm = min(512, t)
    h = _mm(x, lw["w_in"], tm, lw["w_in"].shape[1])
    mix, s_new, van = _mixer(h, nseq, seqlen, pos0, s0, lw["w_s"], lw["b_s"], lw["gate_ln_g"],
                             lw["gate_ln_b"], lw["ret_gn_g"], lw["ret_gn_b"])
    x = _mm_res_ln(mix, lw["w_o"], x, lw["ln1_g"], lw["ln1_b"], alpha, tm)
    if seqlen % tm == 0:
        x = _ca_block(x, mem_k, mem_v, lw["ca_wq"], lw["ca_wo"], lw["ln2_g"], lw["ln2_b"], alpha,
                      nseq, seqlen, tm)
    else:
        qc = _mm(x, lw["ca_wq"], tm, 1024)
        ca = _attn(qc, mem_k, mem_v, nseq, seqlen)
        x = _mm_res_ln(ca, lw["ca_wo"], x, lw["ln2_g"], lw["ln2_b"], alpha, tm)
    a, b, g = _route(x, lw["peer_wq"], lw["peer_subkeys"], min(512, t))
    wg = _wg(a, b, g, 64)
    x = _experts(x, wg, lw["peer_u"], lw["peer_v"], lw["ln3_g"], lw["ln3_b"], alpha, tm, 1024)
    return x, s_new, van


def kernel(x_prompt, x_sample, mem_prompt, cache_mem_k, cache_mem_v, state_ret, w_in, w_s, b_s, gate_ln_g, gate_ln_b, ret_gn_g, ret_gn_b, w_o, ln1_g, ln1_b, ca_wq, ca_wk, ca_wv, ca_wo, ln2_g, ln2_b, peer_wq, peer_subkeys, peer_u, peer_v, ln3_g, ln3_b):
    depth = w_in.shape[0]
    bp, lp, d = x_prompt.shape
    bs, ls, _ = x_sample.shape
    alpha = (2 * depth) ** 0.25
    yp = x_prompt.reshape(bp * lp, d)
    ys = x_sample.reshape(bs * ls, d)
    mem2d = mem_prompt.reshape(bp * N_MEM, d)
    past_len = PAST_LEN
    mk_l, mv_l, sp_l, ss_l, gv_l = [], [], [], [], []
    for l in range(depth):
        lw = {
            "w_in": w_in[l].astype(BF16), "w_s": w_s[l], "b_s": b_s[l],
            "gate_ln_g": gate_ln_g[l], "gate_ln_b": gate_ln_b[l],
            "ret_gn_g": ret_gn_g[l], "ret_gn_b": ret_gn_b[l],
            "w_o": w_o[l].astype(BF16), "ln1_g": ln1_g[l], "ln1_b": ln1_b[l],
            "ca_wq": ca_wq[l].astype(BF16), "ca_wo": ca_wo[l].astype(BF16),
            "ln2_g": ln2_g[l], "ln2_b": ln2_b[l],
            "peer_wq": peer_wq[l].astype(BF16),
            "peer_subkeys": peer_subkeys[l].astype(BF16),
            "peer_u": peer_u[l].astype(BF16), "peer_v": peer_v[l].astype(BF16),
            "ln3_g": ln3_g[l], "ln3_b": ln3_b[l],
        }
        mem_k = _mm(mem2d, ca_wk[l].astype(BF16), 512, 1024)
        mem_v = _mm(mem2d, ca_wv[l].astype(BF16), 512, 1024)
        yp, sp, _ = _layer(yp, bp, lp, 0, mem_k.reshape(bp, N_MEM, d), mem_v.reshape(bp, N_MEM, d),
                           None, lw, alpha)
        ys, ss, gv = _layer(ys, bs, ls, past_len, cache_mem_k[l].reshape(bs, N_MEM, d),
                            cache_mem_v[l].reshape(bs, N_MEM, d), state_ret[l], lw, alpha)
        mk_l.append(mem_k.reshape(bp, N_MEM, CA_HEADS, CA_DH))
        mv_l.append(mem_v.reshape(bp, N_MEM, CA_HEADS, CA_DH))
        sp_l.append(sp)
        ss_l.append(ss)
        gv_l.append(gv.reshape(bs, ls, A_GROUPS, A_CH))
    return (yp.reshape(bp, lp, d), ys.reshape(bs, ls, d), jnp.stack(mk_l), jnp.stack(mv_l),
            jnp.stack(sp_l), jnp.stack(ss_l), jnp.stack(gv_l))
```
